```python
import math
import jax, jax.numpy as jnp
from jax import lax
import numpy as np

D_MODEL = 1024
BATCH = 8
SEQ = 2048
DEPTH = 1
DEC_BATCH = 2
DEC_SEQ = 16384
PAST_LEN = 128

D_CONV = 512
CONV_K = 3
N_DIFF_HEADS = 4
DIFF_HEAD_DIM = 64
D_ATTN = N_DIFF_HEADS * 2 * DIFF_HEAD_DIM
ROT_DIM = DIFF_HEAD_DIM // 4
ROPE_THETA = 500000.0
Q_BLOCK = 128
EPS = 1e-6
IN_COLS = 4 * D_CONV + 4 * D_ATTN + 2 * D_MODEL

kernel_name = "hybrid_conv_diffattn_encoder"


def rms_norm(x, g):
    xf = x.astype(jnp.float32)
    y = xf * lax.rsqrt(jnp.mean(xf * xf, axis=-1, keepdims=True) + EPS)
    return (y * g.astype(jnp.float32)).astype(x.dtype)


def rope_partial(x, pos):
    half = ROT_DIM // 2
    inv = ROPE_THETA ** (-jnp.arange(half, dtype=jnp.float32) / half)
    ang = pos.astype(jnp.float32)[:, None] * inv[None, :]
    cos, sin = jnp.cos(ang), jnp.sin(ang)
    xr = x[..., :ROT_DIM].astype(jnp.float32)
    x1, x2 = xr[..., :half], xr[..., half:]
    rot = jnp.concatenate([x1 * cos - x2 * sin, x2 * cos + x1 * sin], axis=-1).astype(x.dtype)
    return jnp.concatenate([rot, x[..., ROT_DIM:]], axis=-1)


def depthwise_conv_centred(u, w):
    pad = (CONV_K - 1) // 2
    return lax.conv_general_dilated(
        u, w[:, None, :].astype(u.dtype), window_strides=(1,), padding=[(pad, pad)],
        dimension_numbers=('NWC', 'WIO', 'NWC'), feature_group_count=u.shape[-1])


def diff_attention(q, k, v, lam):
    B, H, _, S, Dh = q.shape
    nb = S // Q_BLOCK
    scale = Dh ** -0.5
    qb = q.reshape(B, H, 2, nb, Q_BLOCK, Dh).transpose(3, 0, 1, 2, 4, 5)

    def one_block(qblk):
        s = jnp.einsum('bhcqd,bhckd->bhcqk', qblk, k,
                       preferred_element_type=jnp.float32) * scale
        p = jax.nn.softmax(s, axis=-1)
        a = p[:, :, 0] - lam * p[:, :, 1]
        o = jnp.einsum('bhqk,bhkv->bhqv', a.astype(v.dtype), v,
                       preferred_element_type=jnp.float32)
        return o.astype(v.dtype)

    out = lax.map(one_block, qb)
    return out.transpose(1, 2, 0, 3, 4).reshape(B, H, S, 2 * Dh)


def encoder_layer(x, cond, layer_idx, norm_g, w_ada, b_ada, w_in, conv_w, q_norm_g, k_norm_g,
                  lam_q1, lam_k1, lam_q2, lam_k2, subln_g, w_conv_out, w_attn_out, w_out):
    B, S, _ = x.shape
    H, Dh = N_DIFF_HEADS, DIFF_HEAD_DIM
    mod = jax.nn.silu(cond) @ w_ada + b_ada
    shift, scale, gate = jnp.split(mod[:, None, :], 3, axis=-1)
    h = rms_norm(x, norm_g) * (1 + scale) + shift

    proj = h @ w_in
    sizes = [D_CONV] * 4 + [D_ATTN] * 4 + [D_MODEL] * 2
    cuts = np.cumsum(sizes)[:-1].tolist()
    cb, cc, cx, cz, q, k, v, az, ga, gb = jnp.split(proj, cuts, axis=-1)

    y_conv = cb * depthwise_conv_centred(cc * cx, conv_w)
    y_conv = y_conv * jax.nn.silu(cz)
    branch_a = y_conv @ w_conv_out

    pos = jnp.arange(S)
    q = q.reshape(B, S, H, 2, Dh).transpose(0, 2, 3, 1, 4)
    k = k.reshape(B, S, H, 2, Dh).transpose(0, 2, 3, 1, 4)
    v = v.reshape(B, S, H, 2 * Dh).transpose(0, 2, 1, 3)
    q = rope_partial(rms_norm(q, q_norm_g), pos)
    k = rope_partial(rms_norm(k, k_norm_g), pos)
    lam_init = 0.8 - 0.6 * math.exp(-0.3 * layer_idx)
    lam = (jnp.exp(jnp.sum(lam_q1.astype(jnp.float32) * lam_k1.astype(jnp.float32)))
           - jnp.exp(jnp.sum(lam_q2.astype(jnp.float32) * lam_k2.astype(jnp.float32)))
           + lam_init)
    o = diff_attention(q, k, v, lam)
    o = rms_norm(o, subln_g) * (1.0 - lam_init)
    o = o.transpose(0, 2, 1, 3).reshape(B, S, D_ATTN) * jax.nn.silu(az)
    branch_b = o @ w_attn_out

    merged = jax.nn.sigmoid(ga) * branch_a + jax.nn.sigmoid(gb) * branch_b
    return x + gate * (merged @ w_out)


def setup_inputs(seed: int = 0) -> dict:
    key = jax.random.key(seed)
    ks = jax.random.split(key, 20)
    f32 = jnp.float32
    D = D_MODEL
    nrm = lambda k, shape: jax.random.normal(k, shape, f32)
    return {
        "x_prompt": nrm(ks[0], (BATCH, SEQ, D)),
        "x_sample": nrm(ks[1], (DEC_BATCH, DEC_SEQ, D)),
        "c_prompt": nrm(ks[2], (BATCH, D)),
        "c_sample": nrm(ks[3], (DEC_BATCH, D)),
        "norm_g": 1.0 + 0.02 * nrm(ks[4], (DEPTH, D)),
        "w_ada": 0.5 * D ** -0.5 * nrm(ks[5], (DEPTH, D, 3 * D)),
        "b_ada": 0.01 * nrm(ks[6], (DEPTH, 3 * D)),
        "w_in": D ** -0.5 * nrm(ks[7], (DEPTH, D, IN_COLS)),
        "conv_w": CONV_K ** -0.5 * nrm(ks[8], (DEPTH, CONV_K, D_CONV)),
        "q_norm_g": 1.0 + 0.02 * nrm(ks[9], (DEPTH, DIFF_HEAD_DIM)),
        "k_norm_g": 1.0 + 0.02 * nrm(ks[10], (DEPTH, DIFF_HEAD_DIM)),
        "lam_q1": 0.1 * nrm(ks[11], (DEPTH, DIFF_HEAD_DIM)),
        "lam_k1": 0.1 * nrm(ks[12], (DEPTH, DIFF_HEAD_DIM)),
        "lam_q2": 0.1 * nrm(ks[13], (DEPTH, DIFF_HEAD_DIM)),
        "lam_k2": 0.1 * nrm(ks[14], (DEPTH, DIFF_HEAD_DIM)),
        "subln_g": 1.0 + 0.02 * nrm(ks[15], (DEPTH, 2 * DIFF_HEAD_DIM)),
        "w_conv_out": D_CONV ** -0.5 * nrm(ks[16], (DEPTH, D_CONV, D)),
        "w_attn_out": D_ATTN ** -0.5 * nrm(ks[17], (DEPTH, D_ATTN, D)),
        "w_out": D ** -0.5 * nrm(ks[18], (DEPTH, D, D)),
    }


def reference(x_prompt, x_sample, c_prompt, c_sample, norm_g, w_ada, b_ada, w_in, conv_w,
              q_norm_g, k_norm_g, lam_q1, lam_k1, lam_q2, lam_k2, subln_g,
              w_conv_out, w_attn_out, w_out):
    def trunk(x, cond):
        for l in range(DEPTH):
            x = encoder_layer(x, cond, l, norm_g[l], w_ada[l], b_ada[l], w_in[l], conv_w[l],
                              q_norm_g[l], k_norm_g[l], lam_q1[l], lam_k1[l], lam_q2[l],
                              lam_k2[l], subln_g[l], w_conv_out[l], w_attn_out[l], w_out[l])
        return x

    y_prompt = trunk(x_prompt, c_prompt)
    y_sample = trunk(x_sample, c_sample)
    return (y_prompt, y_sample)
```

```python
import functools
import math

import jax
import jax.numpy as jnp
from jax import lax
from jax.experimental import pallas as pl
from jax.experimental.pallas import tpu as pltpu

F32 = jnp.float32
BF16 = jnp.bfloat16

D_MODEL = 1024
D_CONV = 512
N_HEADS = 4
HEAD_DIM = 64
HEAD_W = 2 * HEAD_DIM
D_ATTN = N_HEADS * HEAD_W
ROT_DIM = HEAD_DIM // 4
ROT_HALF = ROT_DIM // 2
ROPE_THETA = 500000.0
EPS = 1e-6
LOG2E = 1.4426950408889634

_C_CONV = 0
_C_Q = 4 * D_CONV
_C_K = _C_Q + D_ATTN
_C_V = _C_K + D_ATTN
_C_AZ = _C_V + D_ATTN
_C_GA = _C_AZ + D_ATTN
_C_GB = _C_GA + D_MODEL

V7X_VMEM_LIMIT_BYTES = 56 * 1024 * 1024
HALO_ROWS = 8


def _silu(z):
    return z * jax.nn.sigmoid(z)


def _mod_kernel(c_ref, w_ref, b_ref, o_ref):
    s = _silu(c_ref[...])
    o_ref[...] = jnp.dot(s, w_ref[...], preferred_element_type=F32,
                         precision=lax.Precision.HIGHEST) + b_ref[...]


def _mod_call(c_all, w_ada, b_ada):
    rows, d = c_all.shape
    n = w_ada.shape[1]
    tn = 1024
    return pl.pallas_call(
        _mod_kernel,
        grid=(n // tn,),
        in_specs=[pl.BlockSpec((rows, d), lambda j: (0, 0)),
                  pl.BlockSpec((d, tn), lambda j: (0, j)),
                  pl.BlockSpec((1, tn), lambda j: (0, j))],
        out_specs=pl.BlockSpec((rows, tn), lambda j: (0, j)),
        out_shape=jax.ShapeDtypeStruct((rows, n), F32),
        name="adaln_mod",
    )(c_all, w_ada, b_ada.reshape(1, n))


def _rope_kernel(inv_ref, m1_ref, m2_ref, cos_ref, sa_ref, sb_ref):
    ts = cos_ref.shape[0]
    pos = (pl.program_id(0) * ts + lax.broadcasted_iota(jnp.int32, (ts, HEAD_W), 0)).astype(F32)
    ang = pos * inv_ref[...]
    sin = jnp.sin(ang)
    cos_ref[...] = jnp.cos(ang)
    sa_ref[...] = -sin * m1_ref[...]
    sb_ref[...] = sin * m2_ref[...]


def _rope_call(seq):
    lane = jnp.arange(HEAD_W)
    sub = lane % HEAD_DIM
    inv8 = ROPE_THETA ** (-jnp.arange(ROT_HALF, dtype=F32) / ROT_HALF)
    inv = jnp.where(sub < ROT_DIM, inv8[sub % ROT_HALF], 0.0).astype(F32).reshape(1, HEAD_W)
    m1 = (sub < ROT_HALF).astype(F32).reshape(1, HEAD_W)
    m2 = ((sub >= ROT_HALF) & (sub < ROT_DIM)).astype(F32).reshape(1, HEAD_W)
    ts = min(seq, 2048)
    vec = pl.BlockSpec((1, HEAD_W), lambda i: (0, 0))
    tab = pl.BlockSpec((ts, HEAD_W), lambda i: (i, 0))
    return pl.pallas_call(
        _rope_kernel,
        grid=(seq // ts,),
        in_specs=[vec, vec, vec],
        out_specs=[tab, tab, tab],
        out_shape=[jax.ShapeDtypeStruct((seq, HEAD_W), F32)] * 3,
        name="rope_tables",
    )(inv, m1, m2)


def _group_mean_sq(x, g2):
    sq = x * x
    hi = sq.astype(BF16)
    lo = (sq - hi.astype(F32)).astype(BF16)
    return jnp.dot(jnp.concatenate([hi, lo], axis=1), g2, preferred_element_type=F32)


def _qk_norm_rope(x, gain, g2, cos, sa, sb):
    xn = x * lax.rsqrt(_group_mean_sq(x, g2) + EPS) * gain
    return xn * cos + pltpu.roll(xn, HEAD_W - ROT_HALF, 1) * sa + pltpu.roll(xn, ROT_HALF, 1) * sb


def _inproj_kernel(x_ref, xp_ref, xn_ref, mod_ref, ng_ref, win_ref, convw_ref, qg_ref, kg_ref,
                   g2_ref, cos_ref, sa_ref, sb_ref, wco_ref,
                   q_ref, k_ref, v_ref, sz_ref, a_ref, sgb_ref, *, q_scale):
    i = pl.program_id(1)
    last = pl.num_programs(1) - 1
    tm = x_ref.shape[0]
    shift = mod_ref[0:1, :]
    scale1 = 1.0 + mod_ref[1:2, :]
    ng = ng_ref[...]

    def norm_mod(x):
        ms = jnp.mean(x * x, axis=-1, keepdims=True)
        return (x * lax.rsqrt(ms + EPS) * ng) * scale1 + shift

    h = norm_mod(x_ref[...]).astype(BF16)

    def proj(lo, width):
        return jnp.dot(h, win_ref[:, lo:lo + width], preferred_element_type=F32)

    pc = proj(_C_CONV, 4 * D_CONV)
    cb, cc, cx, cz = (pc[:, j * D_CONV:(j + 1) * D_CONV] for j in range(4))
    u = cc * cx
    hh = norm_mod(jnp.concatenate([xp_ref[...], xn_ref[...]], axis=0)).astype(BF16)
    ph = jnp.dot(hh, win_ref[:, D_CONV:3 * D_CONV], preferred_element_type=F32)
    uh = ph[:, :D_CONV] * ph[:, D_CONV:]
    u_prev = jnp.where(i > 0, uh[HALO_ROWS - 1:HALO_ROWS, :], 0.0)
    u_next = jnp.where(i < last, uh[HALO_ROWS:HALO_ROWS + 1, :], 0.0)
    row = lax.broadcasted_iota(jnp.int32, (tm, D_CONV), 0)
    u_dn = jnp.where(row == 0, u_prev, pltpu.roll(u, 1, 0))
    u_up = jnp.where(row == tm - 1, u_next, pltpu.roll(u, tm - 1, 0))
    cw = convw_ref[...]
    y = cb * (cw[0:1, :] * u_dn + cw[1:2, :] * u + cw[2:3, :] * u_up) * _silu(cz)
    branch_a = jnp.dot(y.astype(BF16), wco_ref[...], preferred_element_type=F32)
    a_ref[...] = (jax.nn.sigmoid(proj(_C_GA, D_MODEL)) * branch_a).astype(BF16)
    sgb_ref[...] = jax.nn.sigmoid(proj(_C_GB, D_MODEL)).astype(BF16)

    pv = proj(_C_V, 2 * D_ATTN)
    sz_ref[...] = _silu(pv[:, D_ATTN:]).astype(BF16)
    ones = jnp.ones((tm, HEAD_W), BF16)
    for hd in range(N_HEADS):
        v_ref[:, 2 * hd * HEAD_W:(2 * hd + 1) * HEAD_W] = pv[:, hd * HEAD_W:(hd + 1) * HEAD_W].astype(BF16)
        v_ref[:, (2 * hd + 1) * HEAD_W:(2 * hd + 2) * HEAD_W] = ones

    pqk = proj(_C_Q, 2 * D_ATTN)
    g2 = g2_ref[...]
    cos, sa, sb = cos_ref[...], sa_ref[...], sb_ref[...]
    qg, kg = qg_ref[...], kg_ref[...]
    for hd in range(N_HEADS):
        lo = hd * HEAD_W
        qh = _qk_norm_rope(pqk[:, lo:lo + HEAD_W], qg, g2, cos, sa, sb)
        q_ref[:, lo:lo + HEAD_W] = (qh * q_scale).astype(BF16)
        kh = _qk_norm_rope(pqk[:, D_ATTN + lo:D_ATTN + lo + HEAD_W], kg, g2, cos, sa, sb)
        k_ref[:, lo:lo + HEAD_W] = kh.astype(BF16)


def _inproj_call(x, mod, norm_g, w_in, conv_w, q_norm_g, k_norm_g, tables, w_conv_out, *, tm):
    b, s, d = x.shape
    nt = s // tm
    hb = tm // HALO_ROWS
    lane = jnp.arange(HEAD_W)
    same = (jnp.arange(2 * HEAD_W)[:, None] % HEAD_W) // HEAD_DIM == (lane[None, :] // HEAD_DIM)
    g2 = (same.astype(F32) / HEAD_DIM).astype(BF16)
    tile = lambda w: pl.BlockSpec((None, tm, w), lambda bi, i: (bi, i, 0))
    const = lambda shape: pl.BlockSpec(shape, lambda bi, i: (0,) * len(shape), pipeline_mode=pl.Buffered(1))
    tab = pl.BlockSpec((tm, HEAD_W), lambda bi, i: (i, 0))
    in_specs = [
        tile(d),
        pl.BlockSpec((None, HALO_ROWS, d), lambda bi, i: (bi, jnp.maximum(i * hb - 1, 0), 0)),
        pl.BlockSpec((None, HALO_ROWS, d), lambda bi, i: (bi, jnp.minimum((i + 1) * hb, nt * hb - 1), 0)),
        pl.BlockSpec((None, 3, d), lambda bi, i: (bi, 0, 0)),
        const((1, d)),
        const(w_in.shape),
        const(conv_w.shape),
        const((1, HEAD_W)),
        const((1, HEAD_W)),
        const(g2.shape),
        tab, tab, tab,
        const(w_conv_out.shape),
    ]
    out_specs = [tile(D_ATTN), tile(D_ATTN), tile(2 * D_ATTN), tile(D_ATTN), tile(d), tile(d)]
    out_shape = [jax.ShapeDtypeStruct((b, s, w), BF16)
                 for w in (D_ATTN, D_ATTN, 2 * D_ATTN, D_ATTN, d, d)]
    return pl.pallas_call(
        functools.partial(_inproj_kernel, q_scale=HEAD_DIM ** -0.5 * LOG2E),
        grid=(b, nt),
        in_specs=in_specs,
        out_specs=out_specs,
        out_shape=out_shape,
        compiler_params=pltpu.CompilerParams(
            dimension_semantics=("parallel", "parallel"),
            vmem_limit_bytes=V7X_VMEM_LIMIT_BYTES),
        name="inproj",
    )(x, x, x, mod, norm_g.reshape(1, d), w_in, conv_w,
      jnp.tile(q_norm_g, 2).reshape(1, HEAD_W), jnp.tile(k_norm_g, 2).reshape(1, HEAD_W),
      g2, *tables, w_conv_out)


def _attn_kernel(q_ref, k_ref, v_ref, sz_ref, lq1_ref, lk1_ref, lq2_ref, lk2_ref, sg_ref,
                 o_ref, acc_ref, m_ref, *, tk, lam_init):
    tq = q_ref.shape[0]
    nkv = k_ref.shape[0] // tk
    q = q_ref[...]
    lane = lax.broadcasted_iota(jnp.int32, (tq, HEAD_W), 1)
    zero = jnp.zeros_like(q)
    qz = jnp.concatenate([jnp.where(lane < HEAD_DIM, q, zero), jnp.where(lane >= HEAD_DIM, q, zero)], axis=0)
    m_ref[...] = jnp.full(m_ref.shape, -jnp.inf, F32)
    acc_ref[...] = jnp.zeros(acc_ref.shape, F32)

    def body(j, carry):
        off = pl.multiple_of(j * tk, tk)
        kc = k_ref[pl.ds(off, tk), :]
        vc = v_ref[pl.ds(off, tk), :]
        s = lax.dot_general(qz, kc, (((1,), (1,)), ((), ())), preferred_element_type=F32)
        m_prev = m_ref[...]
        m_new = jnp.maximum(m_prev, jnp.max(s, axis=1, keepdims=True))
        alpha = jnp.exp2(m_prev - m_new)
        p = jnp.exp2(s - m_new)
        acc_ref[...] = alpha * acc_ref[...] + jnp.dot(p.astype(BF16), vc, preferred_element_type=F32)
        m_ref[...] = m_new
        return carry

    lax.fori_loop(0, nkv, body, 0)

    acc = acc_ref[...]
    o = acc[:, :HEAD_W] / acc[:, HEAD_W:]
    lam = (jnp.exp(jnp.sum(lq1_ref[...] * lk1_ref[...], axis=1, keepdims=True))
           - jnp.exp(jnp.sum(lq2_ref[...] * lk2_ref[...], axis=1, keepdims=True)) + lam_init)
    od = o[:tq] - lam * o[tq:]
    ms = jnp.mean(od * od, axis=-1, keepdims=True)
    on = od * lax.rsqrt(ms + EPS) * sg_ref[...] * (1.0 - lam_init)
    o_ref[...] = (on * sz_ref[...].astype(F32)).astype(BF16)


def _attn_call(q, k, v_ext, sz, lam_q1, lam_k1, lam_q2, lam_k2, subln_g, *, tq, tk, lam_init):
    b, s, _ = q.shape
    qtile = pl.BlockSpec((None, tq, HEAD_W), lambda bi, h, i: (bi, i, h))
    vec = lambda w: pl.BlockSpec((1, w), lambda bi, h, i: (0, 0))
    return pl.pallas_call(
        functools.partial(_attn_kernel, tk=tk, lam_init=lam_init),
        grid=(b, N_HEADS, s // tq),
        in_specs=[qtile,
                  pl.BlockSpec((None, s, HEAD_W), lambda bi, h, i: (bi, 0, h)),
                  pl.BlockSpec((None, s, 2 * HEAD_W), lambda bi, h, i: (bi, 0, h)),
                  qtile,
                  vec(HEAD_DIM), vec(HEAD_DIM), vec(HEAD_DIM), vec(HEAD_DIM), vec(HEAD_W)],
        out_specs=qtile,
        out_shape=jax.ShapeDtypeStruct((b, s, D_ATTN), BF16),
        scratch_shapes=[pltpu.VMEM((2 * tq, 2 * HEAD_W), F32), pltpu.VMEM((2 * tq, 1), F32)],
        compiler_params=pltpu.CompilerParams(
            dimension_semantics=("parallel", "parallel", "parallel"),
            vmem_limit_bytes=V7X_VMEM_LIMIT_BYTES),
        name="diff_attn",
    )(q, k, v_ext, sz, lam_q1.reshape(1, -1), lam_k1.reshape(1, -1), lam_q2.reshape(1, -1),
      lam_k2.reshape(1, -1), subln_g.reshape(1, -1))


def _out_kernel(x_ref, og_ref, a_ref, sgb_ref, mod_ref, wao_ref, wo_ref, y_ref):
    branch_b = jnp.dot(og_ref[...], wao_ref[...], preferred_element_type=F32)
    merged = a_ref[...].astype(F32) + sgb_ref[...].astype(F32) * branch_b
    z = jnp.dot(merged.astype(BF16), wo_ref[...], preferred_element_type=F32)
    y_ref[...] = x_ref[...] + mod_ref[2:3, :] * z


def _out_call(x, og, a, sgb, mod, w_attn_out, w_out, *, tm):
    b, s, d = x.shape
    tile = lambda w: pl.BlockSpec((None, tm, w), lambda bi, i: (bi, i, 0))
    const = lambda shape: pl.BlockSpec(shape, lambda bi, i: (0,) * len(shape))
    return pl.pallas_call(
        _out_kernel,
        grid=(b, s // tm),
        in_specs=[tile(d), tile(D_ATTN), tile(d), tile(d),
                  pl.BlockSpec((None, 3, d), lambda bi, i: (bi, 0, 0)),
                  const(w_attn_out.shape), const(w_out.shape)],
        out_specs=tile(d),
        out_shape=jax.ShapeDtypeStruct((b, s, d), F32),
        compiler_params=pltpu.CompilerParams(
            dimension_semantics=("parallel", "parallel"),
            vmem_limit_bytes=V7X_VMEM_LIMIT_BYTES),
        name="out_proj",
    )(x, og, a, sgb, mod, w_attn_out, w_out)


def _tiles(seq):
    return dict(tm=min(seq, 512), tq=min(seq, 256), tk=min(seq, 512))


def _layer(x, mod, layer_idx, tables, norm_g, w_in, conv_w, q_norm_g, k_norm_g, lam_q1, lam_k1, lam_q2,
           lam_k2, subln_g, w_conv_out, w_attn_out, w_out):
    t = _tiles(x.shape[1])
    lam_init = 0.8 - 0.6 * math.exp(-0.3 * layer_idx)
    q, k, v_ext, sz, a, sgb = _inproj_call(x, mod, norm_g, w_in, conv_w, q_norm_g, k_norm_g, tables,
                                           w_conv_out, tm=t["tm"])
    og = _attn_call(q, k, v_ext, sz, lam_q1, lam_k1, lam_q2, lam_k2, subln_g,
                    tq=t["tq"], tk=t["tk"], lam_init=lam_init)
    return _out_call(x, og, a, sgb, mod, w_attn_out, w_out, tm=t["tm"])


def kernel(x_prompt, x_sample, c_prompt, c_sample, norm_g, w_ada, b_ada, w_in, conv_w, q_norm_g, k_norm_g,
           lam_q1, lam_k1, lam_q2, lam_k2, subln_g, w_conv_out, w_attn_out, w_out):
    depth = norm_g.shape[0]
    groups = [(x_prompt, c_prompt), (x_sample, c_sample)]
    n_rows = sum(c.shape[0] for _, c in groups)
    pad = -n_rows % 8
    c_all = jnp.concatenate([c for _, c in groups] + [jnp.zeros((pad, D_MODEL), F32)], axis=0)
    tables = _rope_call(max(x.shape[1] for x, _ in groups))
    xs = [x for x, _ in groups]
    for l in range(depth):
        mod_all = _mod_call(c_all, w_ada[l], b_ada[l])
        w_in_b, wco_b, wao_b, wo_b = (w[l].astype(BF16) for w in (w_in, w_conv_out, w_attn_out, w_out))
        row = 0
        for gi, (_, c) in enumerate(groups):
            nb = c.shape[0]
            mod = mod_all[row:row + nb].reshape(nb, 3, D_MODEL)
            row += nb
            xs[gi] = _layer(xs[gi], mod, l, tables, norm_g[l], w_in_b, conv_w[l],
                            q_norm_g[l], k_norm_g[l], lam_q1[l], lam_k1[l], lam_q2[l], lam_k2[l],
                            subln_g[l], wco_b, wao_b, wo_b)
    return tuple(xs)
```

```python
import functools
import math

import jax
import jax.numpy as jnp
from jax import lax
from jax.experimental import pallas as pl
from jax.experimental.pallas import tpu as pltpu

F32 = jnp.float32
BF16 = jnp.bfloat16

D_MODEL = 1024
D_CONV = 512
N_HEADS = 4
HEAD_DIM = 64
HEAD_W = 2 * HEAD_DIM
D_ATTN = N_HEADS * HEAD_W
ROT_DIM = HEAD_DIM // 4
ROT_HALF = ROT_DIM // 2
ROPE_THETA = 500000.0
EPS = 1e-6
LOG2E = 1.4426950408889634

_C_CONV = 0
_C_Q = 4 * D_CONV
_C_K = _C_Q + D_ATTN
_C_V = _C_K + D_ATTN
_C_AZ = _C_V + D_ATTN
_C_GA = _C_AZ + D_ATTN
_C_GB = _C_GA + D_MODEL

V7X_VMEM_LIMIT_BYTES = 56 * 1024 * 1024
HALO_ROWS = 8


def _silu(z):
    return z * jax.nn.sigmoid(z)


def _mod_kernel(c_ref, w_ref, b_ref, o_ref):
    s = _silu(c_ref[...])
    o_ref[...] = jnp.dot(s, w_ref[...], preferred_element_type=F32,
                         precision=lax.Precision.HIGHEST) + b_ref[...]


def _mod_call(c_all, w_ada, b_ada):
    rows, d = c_all.shape
    n = w_ada.shape[1]
    tn = 1024
    return pl.pallas_call(
        _mod_kernel,
        grid=(n // tn,),
        in_specs=[pl.BlockSpec((rows, d), lambda j: (0, 0)),
                  pl.BlockSpec((d, tn), lambda j: (0, j)),
                  pl.BlockSpec((1, tn), lambda j: (0, j))],
        out_specs=pl.BlockSpec((rows, tn), lambda j: (0, j)),
        out_shape=jax.ShapeDtypeStruct((rows, n), F32),
        name="adaln_mod",
    )(c_all, w_ada, b_ada.reshape(1, n))


def _rope_kernel(inv_ref, m1_ref, m2_ref, cos_ref, sa_ref, sb_ref):
    ts = cos_ref.shape[0]
    pos = (pl.program_id(0) * ts + lax.broadcasted_iota(jnp.int32, (ts, HEAD_W), 0)).astype(F32)
    ang = pos * inv_ref[...]
    sin = jnp.sin(ang)
    cos_ref[...] = jnp.cos(ang)
    sa_ref[...] = -sin * m1_ref[...]
    sb_ref[...] = sin * m2_ref[...]


def _rope_call(seq):
    lane = jnp.arange(HEAD_W)
    sub = lane % HEAD_DIM
    inv8 = ROPE_THETA ** (-jnp.arange(ROT_HALF, dtype=F32) / ROT_HALF)
    inv = jnp.where(sub < ROT_DIM, inv8[sub % ROT_HALF], 0.0).astype(F32).reshape(1, HEAD_W)
    m1 = (sub < ROT_HALF).astype(F32).reshape(1, HEAD_W)
    m2 = ((sub >= ROT_HALF) & (sub < ROT_DIM)).astype(F32).reshape(1, HEAD_W)
    ts = min(seq, 2048)
    vec = pl.BlockSpec((1, HEAD_W), lambda i: (0, 0))
    tab = pl.BlockSpec((ts, HEAD_W), lambda i: (i, 0))
    return pl.pallas_call(
        _rope_kernel,
        grid=(seq // ts,),
        in_specs=[vec, vec, vec],
        out_specs=[tab, tab, tab],
        out_shape=[jax.ShapeDtypeStruct((seq, HEAD_W), F32)] * 3,
        name="rope_tables",
    )(inv, m1, m2)


def _group_mean_sq(x, g2):
    sq = x * x
    hi = sq.astype(BF16)
    lo = (sq - hi.astype(F32)).astype(BF16)
    return jnp.dot(jnp.concatenate([hi, lo], axis=1), g2, preferred_element_type=F32)


def _qk_norm_rope(x, gain, g2, cos, sa, sb):
    xn = x * lax.rsqrt(_group_mean_sq(x, g2) + EPS) * gain
    return xn * cos + pltpu.roll(xn, HEAD_W - ROT_HALF, 1) * sa + pltpu.roll(xn, ROT_HALF, 1) * sb


def _inproj_kernel(x_ref, xp_ref, xn_ref, mod_ref, ng_ref, win_ref, convw_ref, qg_ref, kg_ref,
                   g2_ref, cos_ref, sa_ref, sb_ref, wco_ref,
                   q_ref, k_ref, v_ref, sz_ref, a_ref, sgb_ref, *, q_scale):
    i = pl.program_id(1)
    last = pl.num_programs(1) - 1
    tm = x_ref.shape[0]
    shift = mod_ref[0:1, :]
    scale1 = 1.0 + mod_ref[1:2, :]
    ng = ng_ref[...]

    def norm_mod(x):
        ms = jnp.mean(x * x, axis=-1, keepdims=True)
        return (x * lax.rsqrt(ms + EPS) * ng) * scale1 + shift

    h = norm_mod(x_ref[...]).astype(BF16)

    def proj(lo, width):
        return jnp.dot(h, win_ref[:, lo:lo + width], preferred_element_type=F32)

    pc = proj(_C_CONV, 4 * D_CONV)
    cb, cc, cx, cz = (pc[:, j * D_CONV:(j + 1) * D_CONV] for j in range(4))
    u = cc * cx
    hh = norm_mod(jnp.concatenate([xp_ref[...], xn_ref[...]], axis=0)).astype(BF16)
    ph = jnp.dot(hh, win_ref[:, D_CONV:3 * D_CONV], preferred_element_type=F32)
    uh = ph[:, :D_CONV] * ph[:, D_CONV:]
    u_prev = jnp.where(i > 0, uh[HALO_ROWS - 1:HALO_ROWS, :], 0.0)
    u_next = jnp.where(i < last, uh[HALO_ROWS:HALO_ROWS + 1, :], 0.0)
    row = lax.broadcasted_iota(jnp.int32, (tm, D_CONV), 0)
    u_dn = jnp.where(row == 0, u_prev, pltpu.roll(u, 1, 0))
    u_up = jnp.where(row == tm - 1, u_next, pltpu.roll(u, tm - 1, 0))
    cw = convw_ref[...]
    y = cb * (cw[0:1, :] * u_dn + cw[1:2, :] * u + cw[2:3, :] * u_up) * _silu(cz)
    branch_a = jnp.dot(y.astype(BF16), wco_ref[...], preferred_element_type=F32)
    a_ref[...] = (jax.nn.sigmoid(proj(_C_GA, D_MODEL)) * branch_a).astype(BF16)
    sgb_ref[...] = jax.nn.sigmoid(proj(_C_GB, D_MODEL)).astype(BF16)

    pv = proj(_C_V, 2 * D_ATTN)
    sz_ref[...] = _silu(pv[:, D_ATTN:]).astype(BF16)
    ones = jnp.ones((tm, HEAD_W), BF16)
    for hd in range(N_HEADS):
        v_ref[:, 2 * hd * HEAD_W:(2 * hd + 1) * HEAD_W] = pv[:, hd * HEAD_W:(hd + 1) * HEAD_W].astype(BF16)
        v_ref[:, (2 * hd + 1) * HEAD_W:(2 * hd + 2) * HEAD_W] = ones

    pqk = proj(_C_Q, 2 * D_ATTN)
    g2 = g2_ref[...]
    cos, sa, sb = cos_ref[...], sa_ref[...], sb_ref[...]
    qg, kg = qg_ref[...], kg_ref[...]
    for hd in range(N_HEADS):
        lo = hd * HEAD_W
        qh = _qk_norm_rope(pqk[:, lo:lo + HEAD_W], qg, g2, cos, sa, sb)
        q_ref[:, lo:lo + HEAD_W] = (qh * q_scale).astype(BF16)
        kh = _qk_norm_rope(pqk[:, D_ATTN + lo:D_ATTN + lo + HEAD_W], kg, g2, cos, sa, sb)
        k_ref[:, lo:lo + HEAD_W] = kh.astype(BF16)


def _inproj_call(x, mod, norm_g, w_in, conv_w, q_norm_g, k_norm_g, tables, w_conv_out, *, tm):
    b, s, d = x.shape
    nt = s // tm
    hb = tm // HALO_ROWS
    lane = jnp.arange(HEAD_W)
    same = (jnp.arange(2 * HEAD_W)[:, None] % HEAD_W) // HEAD_DIM == (lane[None, :] // HEAD_DIM)
    g2 = (same.astype(F32) / HEAD_DIM).astype(BF16)
    tile = lambda w: pl.BlockSpec((None, tm, w), lambda bi, i: (bi, i, 0))
    const = lambda shape: pl.BlockSpec(shape, lambda bi, i: (0,) * len(shape), pipeline_mode=pl.Buffered(1))
    tab = pl.BlockSpec((tm, HEAD_W), lambda bi, i: (i, 0))
    in_specs = [
        tile(d),
        pl.BlockSpec((None, HALO_ROWS, d), lambda bi, i: (bi, jnp.maximum(i * hb - 1, 0), 0)),
        pl.BlockSpec((None, HALO_ROWS, d), lambda bi, i: (bi, jnp.minimum((i + 1) * hb, nt * hb - 1), 0)),
        pl.BlockSpec((None, 3, d), lambda bi, i: (bi, 0, 0)),
        const((1, d)),
        const(w_in.shape),
        const(conv_w.shape),
        const((1, HEAD_W)),
        const((1, HEAD_W)),
        const(g2.shape),
        tab, tab, tab,
        const(w_conv_out.shape),
    ]
    out_specs = [tile(D_ATTN), tile(D_ATTN), tile(2 * D_ATTN), tile(D_ATTN), tile(d), tile(d)]
    out_shape = [jax.ShapeDtypeStruct((b, s, w), BF16)
                 for w in (D_ATTN, D_ATTN, 2 * D_ATTN, D_ATTN, d, d)]
    return pl.pallas_call(
        functools.partial(_inproj_kernel, q_scale=HEAD_DIM ** -0.5 * LOG2E),
        grid=(b, nt),
        in_specs=in_specs,
        out_specs=out_specs,
        out_shape=out_shape,
        compiler_params=pltpu.CompilerParams(
            dimension_semantics=("parallel", "parallel"),
            vmem_limit_bytes=V7X_VMEM_LIMIT_BYTES),
        name="inproj",
    )(x, x, x, mod, norm_g.reshape(1, d), w_in, conv_w,
      jnp.tile(q_norm_g, 2).reshape(1, HEAD_W), jnp.tile(k_norm_g, 2).reshape(1, HEAD_W),
      g2, *tables, w_conv_out)


def _attn_kernel(q_ref, k_ref, v_ref, sz_ref, lq1_ref, lk1_ref, lq2_ref, lk2_ref, sg_ref,
                 o_ref, acc_ref, m_ref, qz_ref, s0_ref, s1_ref, *, tk, lam_init):
    tq = q_ref.shape[0]
    nkv = k_ref.shape[0] // tk
    q = q_ref[...]
    lane = lax.broadcasted_iota(jnp.int32, (tq, HEAD_W), 1)
    zero = jnp.zeros_like(q)
    qz_ref[0:tq, :] = jnp.where(lane < HEAD_DIM, q, zero)
    qz_ref[tq:2 * tq, :] = jnp.where(lane >= HEAD_DIM, q, zero)
    m_ref[...] = jnp.full(m_ref.shape, -jnp.inf, F32)
    acc_ref[...] = jnp.zeros(acc_ref.shape, F32)

    def scores(j, s_ref):
        off = pl.multiple_of(j * tk, tk)
        s_ref[...] = lax.dot_general(qz_ref[...], k_ref[pl.ds(off, tk), :], (((1,), (1,)), ((), ())),
                                     preferred_element_type=F32)

    def accumulate(j, s_ref):
        off = pl.multiple_of(j * tk, tk)
        s = s_ref[...]
        m_prev = m_ref[...]
        m_new = jnp.maximum(m_prev, jnp.max(s, axis=1, keepdims=True))
        alpha = jnp.exp2(m_prev - m_new)
        p = jnp.exp2(s - m_new).astype(BF16)
        acc_ref[...] = alpha * acc_ref[...] + jnp.dot(p, v_ref[pl.ds(off, tk), :], preferred_element_type=F32)
        m_ref[...] = m_new

    scores(0, s0_ref)

    def body(jj, carry):
        j = 2 * jj
        scores(j + 1, s1_ref)
        accumulate(j, s0_ref)
        scores(jnp.minimum(j + 2, nkv - 1), s0_ref)
        accumulate(j + 1, s1_ref)
        return carry

    lax.fori_loop(0, nkv // 2, body, 0)

    acc = acc_ref[...]
    o = acc[:, :HEAD_W] / acc[:, HEAD_W:]
    lam = (jnp.exp(jnp.sum(lq1_ref[...] * lk1_ref[...], axis=1, keepdims=True))
           - jnp.exp(jnp.sum(lq2_ref[...] * lk2_ref[...], axis=1, keepdims=True)) + lam_init)
    od = o[:tq] - lam * o[tq:]
    ms = jnp.mean(od * od, axis=-1, keepdims=True)
    on = od * lax.rsqrt(ms + EPS) * sg_ref[...] * (1.0 - lam_init)
    o_ref[...] = (on * sz_ref[...].astype(F32)).astype(BF16)


def _attn_call(q, k, v_ext, sz, lam_q1, lam_k1, lam_q2, lam_k2, subln_g, *, tq, tk, lam_init):
    b, s, _ = q.shape
    assert s % tq == 0 and s % (2 * tk) == 0, (s, tq, tk)
    qtile = pl.BlockSpec((None, tq, HEAD_W), lambda bi, h, i: (bi, i, h))
    vec = lambda w: pl.BlockSpec((1, w), lambda bi, h, i: (0, 0))
    return pl.pallas_call(
        functools.partial(_attn_kernel, tk=tk, lam_init=lam_init),
        grid=(b, N_HEADS, s // tq),
        in_specs=[qtile,
                  pl.BlockSpec((None, s, HEAD_W), lambda bi, h, i: (bi, 0, h)),
                  pl.BlockSpec((None, s, 2 * HEAD_W), lambda bi, h, i: (bi, 0, h)),
                  qtile,
                  vec(HEAD_DIM), vec(HEAD_DIM), vec(HEAD_DIM), vec(HEAD_DIM), vec(HEAD_W)],
        out_specs=qtile,
        out_shape=jax.ShapeDtypeStruct((b, s, D_ATTN), BF16),
        scratch_shapes=[pltpu.VMEM((2 * tq, 2 * HEAD_W), F32), pltpu.VMEM((2 * tq, 1), F32),
                        pltpu.VMEM((2 * tq, HEAD_W), BF16),
                        pltpu.VMEM((2 * tq, tk), F32), pltpu.VMEM((2 * tq, tk), F32)],
        compiler_params=pltpu.CompilerParams(
            dimension_semantics=("parallel", "parallel", "parallel"),
            vmem_limit_bytes=V7X_VMEM_LIMIT_BYTES),
        name="diff_attn",
    )(q, k, v_ext, sz, lam_q1.reshape(1, -1), lam_k1.reshape(1, -1), lam_q2.reshape(1, -1),
      lam_k2.reshape(1, -1), subln_g.reshape(1, -1))


def _out_kernel(x_ref, og_ref, a_ref, sgb_ref, mod_ref, wao_ref, wo_ref, y_ref):
    branch_b = jnp.dot(og_ref[...], wao_ref[...], preferred_element_type=F32)
    merged = a_ref[...].astype(F32) + sgb_ref[...].astype(F32) * branch_b
    z = jnp.dot(merged.astype(BF16), wo_ref[...], preferred_element_type=F32)
    y_ref[...] = x_ref[...] + mod_ref[2:3, :] * z


def _out_call(x, og, a, sgb, mod, w_attn_out, w_out, *, tm):
    b, s, d = x.shape
    tile = lambda w: pl.BlockSpec((None, tm, w), lambda bi, i: (bi, i, 0))
    const = lambda shape: pl.BlockSpec(shape, lambda bi, i: (0,) * len(shape))
    return pl.pallas_call(
        _out_kernel,
        grid=(b, s // tm),
        in_specs=[tile(d), tile(D_ATTN), tile(d), tile(d),
                  pl.BlockSpec((None, 3, d), lambda bi, i: (bi, 0, 0)),
                  const(w_attn_out.shape), const(w_out.shape)],
        out_specs=tile(d),
        out_shape=jax.ShapeDtypeStruct((b, s, d), F32),
        compiler_params=pltpu.CompilerParams(
            dimension_semantics=("parallel", "parallel"),
            vmem_limit_bytes=V7X_VMEM_LIMIT_BYTES),
        name="out_proj",
    )(x, og, a, sgb, mod, w_attn_out, w_out)


def _tiles(seq):
    return dict(tm=min(seq, 512), tq=min(seq, 256), tk=min(seq // 2, 2048))


def _layer(x, mod, layer_idx, tables, norm_g, w_in, conv_w, q_norm_g, k_norm_g, lam_q1, lam_k1, lam_q2,
           lam_k2, subln_g, w_conv_out, w_attn_out, w_out):
    t = _tiles(x.shape[1])
    lam_init = 0.8 - 0.6 * math.exp(-0.3 * layer_idx)
    q, k, v_ext, sz, a, sgb = _inproj_call(x, mod, norm_g, w_in, conv_w, q_norm_g, k_norm_g, tables,
                                           w_conv_out, tm=t["tm"])
    og = _attn_call(q, k, v_ext, sz, lam_q1, lam_k1, lam_q2, lam_k2, subln_g,
                    tq=t["tq"], tk=t["tk"], lam_init=lam_init)
    return _out_call(x, og, a, sgb, mod, w_attn_out, w_out, tm=t["tm"])


def kernel(x_prompt, x_sample, c_prompt, c_sample, norm_g, w_ada, b_ada, w_in, conv_w, q_norm_g, k_norm_g,
           lam_q1, lam_k1, lam_q2, lam_k2, subln_g, w_conv_out, w_attn_out, w_out):
    depth = norm_g.shape[0]
    groups = [(x_prompt, c_prompt), (x_sample, c_sample)]
    n_rows = sum(c.shape[0] for _, c in groups)
    pad = -n_rows % 8
    c_all = jnp.concatenate([c for _, c in groups] + [jnp.zeros((pad, D_MODEL), F32)], axis=0)
    tables = _rope_call(max(x.shape[1] for x, _ in groups))
    xs = [x for x, _ in groups]
    for l in range(depth):
        mod_all = _mod_call(c_all, w_ada[l], b_ada[l])
        w_in_b, wco_b, wao_b, wo_b = (w[l].astype(BF16) for w in (w_in, w_conv_out, w_attn_out, w_out))
        row = 0
        for gi, (_, c) in enumerate(groups):
            nb = c.shape[0]
            mod = mod_all[row:row + nb].reshape(nb, 3, D_MODEL)
            row += nb
            xs[gi] = _layer(xs[gi], mod, l, tables, norm_g[l], w_in_b, conv_w[l],
                            q_norm_g[l], k_norm_g[l], lam_q1[l], lam_k1[l], lam_q2[l], lam_k2[l],
                            subln_g[l], wco_b, wao_b, wo_b)
    return tuple(xs)
```

```python
import functools
import math

import jax
import jax.numpy as jnp
from jax import lax
from jax.experimental import pallas as pl
from jax.experimental.pallas import tpu as pltpu

F32 = jnp.float32
BF16 = jnp.bfloat16

D_MODEL = 1024
D_CONV = 512
N_HEADS = 4
HEAD_DIM = 64
HEAD_W = 2 * HEAD_DIM
D_ATTN = N_HEADS * HEAD_W
ROT_DIM = HEAD_DIM // 4
ROT_HALF = ROT_DIM // 2
ROPE_THETA = 500000.0
EPS = 1e-6
LOG2E = 1.4426950408889634

_C_CONV = 0
_C_Q = 4 * D_CONV
_C_K = _C_Q + D_ATTN
_C_V = _C_K + D_ATTN
_C_AZ = _C_V + D_ATTN
_C_GA = _C_AZ + D_ATTN
_C_GB = _C_GA + D_MODEL

V7X_VMEM_LIMIT_BYTES = 56 * 1024 * 1024
HALO_ROWS = 8


def _silu(z):
    return z * jax.nn.sigmoid(z)


def _mod_kernel(c_ref, w_ref, b_ref, o_ref):
    s = _silu(c_ref[...])
    o_ref[...] = jnp.dot(s, w_ref[...], preferred_element_type=F32,
                         precision=lax.Precision.HIGHEST) + b_ref[...]


def _mod_call(c_all, w_ada, b_ada):
    rows, d = c_all.shape
    n = w_ada.shape[1]
    tn = 1024
    return pl.pallas_call(
        _mod_kernel,
        grid=(n // tn,),
        in_specs=[pl.BlockSpec((rows, d), lambda j: (0, 0)),
                  pl.BlockSpec((d, tn), lambda j: (0, j)),
                  pl.BlockSpec((1, tn), lambda j: (0, j))],
        out_specs=pl.BlockSpec((rows, tn), lambda j: (0, j)),
        out_shape=jax.ShapeDtypeStruct((rows, n), F32),
        name="adaln_mod",
    )(c_all, w_ada, b_ada.reshape(1, n))


def _rope_kernel(inv_ref, m1_ref, m2_ref, cos_ref, sa_ref, sb_ref):
    ts = cos_ref.shape[0]
    pos = (pl.program_id(0) * ts + lax.broadcasted_iota(jnp.int32, (ts, HEAD_W), 0)).astype(F32)
    ang = pos * inv_ref[...]
    sin = jnp.sin(ang)
    cos_ref[...] = jnp.cos(ang)
    sa_ref[...] = -sin * m1_ref[...]
    sb_ref[...] = sin * m2_ref[...]


def _rope_call(seq):
    lane = jnp.arange(HEAD_W)
    sub = lane % HEAD_DIM
    inv8 = ROPE_THETA ** (-jnp.arange(ROT_HALF, dtype=F32) / ROT_HALF)
    inv = jnp.where(sub < ROT_DIM, inv8[sub % ROT_HALF], 0.0).astype(F32).reshape(1, HEAD_W)
    m1 = (sub < ROT_HALF).astype(F32).reshape(1, HEAD_W)
    m2 = ((sub >= ROT_HALF) & (sub < ROT_DIM)).astype(F32).reshape(1, HEAD_W)
    ts = min(seq, 2048)
    vec = pl.BlockSpec((1, HEAD_W), lambda i: (0, 0))
    tab = pl.BlockSpec((ts, HEAD_W), lambda i: (i, 0))
    return pl.pallas_call(
        _rope_kernel,
        grid=(seq // ts,),
        in_specs=[vec, vec, vec],
        out_specs=[tab, tab, tab],
        out_shape=[jax.ShapeDtypeStruct((seq, HEAD_W), F32)] * 3,
        name="rope_tables",
    )(inv, m1, m2)


def _group_mean_sq(x, g2):
    sq = x * x
    hi = sq.astype(BF16)
    lo = (sq - hi.astype(F32)).astype(BF16)
    return jnp.dot(jnp.concatenate([hi, lo], axis=1), g2, preferred_element_type=F32)


def _qk_norm_rope(x, gain, g2, cos, sa, sb):
    xn = x * lax.rsqrt(_group_mean_sq(x, g2) + EPS) * gain
    return xn * cos + pltpu.roll(xn, HEAD_W - ROT_HALF, 1) * sa + pltpu.roll(xn, ROT_HALF, 1) * sb


def _inproj_kernel(x_ref, xp_ref, xn_ref, mod_ref, ng_ref, win_ref, convw_ref, qg_ref, kg_ref,
                   g2_ref, cos_ref, sa_ref, sb_ref, wco_ref,
                   q_ref, k_ref, v_ref, sz_ref, a_ref, sgb_ref, *, q_scale):
    i = pl.program_id(1)
    last = pl.num_programs(1) - 1
    tm = x_ref.shape[0]
    shift = mod_ref[0:1, :]
    scale1 = 1.0 + mod_ref[1:2, :]
    ng = ng_ref[...]

    def norm_mod(x):
        ms = jnp.mean(x * x, axis=-1, keepdims=True)
        return (x * lax.rsqrt(ms + EPS) * ng) * scale1 + shift

    h = norm_mod(x_ref[...]).astype(BF16)

    def proj(lo, width):
        return jnp.dot(h, win_ref[:, lo:lo + width], preferred_element_type=F32)

    pc = proj(_C_CONV, 4 * D_CONV)
    cb, cc, cx, cz = (pc[:, j * D_CONV:(j + 1) * D_CONV] for j in range(4))
    u = cc * cx
    hh = norm_mod(jnp.concatenate([xp_ref[...], xn_ref[...]], axis=0)).astype(BF16)
    ph = jnp.dot(hh, win_ref[:, D_CONV:3 * D_CONV], preferred_element_type=F32)
    uh = ph[:, :D_CONV] * ph[:, D_CONV:]
    u_prev = jnp.where(i > 0, uh[HALO_ROWS - 1:HALO_ROWS, :], 0.0)
    u_next = jnp.where(i < last, uh[HALO_ROWS:HALO_ROWS + 1, :], 0.0)
    row = lax.broadcasted_iota(jnp.int32, (tm, D_CONV), 0)
    u_dn = jnp.where(row == 0, u_prev, pltpu.roll(u, 1, 0))
    u_up = jnp.where(row == tm - 1, u_next, pltpu.roll(u, tm - 1, 0))
    cw = convw_ref[...]
    y = cb * (cw[0:1, :] * u_dn + cw[1:2, :] * u + cw[2:3, :] * u_up) * _silu(cz)
    branch_a = jnp.dot(y.astype(BF16), wco_ref[...], preferred_element_type=F32)
    a_ref[...] = (jax.nn.sigmoid(proj(_C_GA, D_MODEL)) * branch_a).astype(BF16)
    sgb_ref[...] = jax.nn.sigmoid(proj(_C_GB, D_MODEL)).astype(BF16)

    pv = proj(_C_V, 2 * D_ATTN)
    sz_ref[...] = _silu(pv[:, D_ATTN:]).astype(BF16)
    ones = jnp.ones((tm, HEAD_W), BF16)
    for hd in range(N_HEADS):
        v_ref[:, 2 * hd * HEAD_W:(2 * hd + 1) * HEAD_W] = pv[:, hd * HEAD_W:(hd + 1) * HEAD_W].astype(BF16)
        v_ref[:, (2 * hd + 1) * HEAD_W:(2 * hd + 2) * HEAD_W] = ones

    pqk = proj(_C_Q, 2 * D_ATTN)
    g2 = g2_ref[...]
    cos, sa, sb = cos_ref[...], sa_ref[...], sb_ref[...]
    qg, kg = qg_ref[...], kg_ref[...]
    for hd in range(N_HEADS):
        lo = hd * HEAD_W
        qh = _qk_norm_rope(pqk[:, lo:lo + HEAD_W], qg, g2, cos, sa, sb)
        q_ref[:, lo:lo + HEAD_W] = (qh * q_scale).astype(BF16)
        kh = _qk_norm_rope(pqk[:, D_ATTN + lo:D_ATTN + lo + HEAD_W], kg, g2, cos, sa, sb)
        k_ref[:, lo:lo + HEAD_W] = kh.astype(BF16)


def _inproj_call(x, mod, norm_g, w_in, conv_w, q_norm_g, k_norm_g, tables, w_conv_out, *, tm):
    b, s, d = x.shape
    nt = s // tm
    hb = tm // HALO_ROWS
    lane = jnp.arange(HEAD_W)
    same = (jnp.arange(2 * HEAD_W)[:, None] % HEAD_W) // HEAD_DIM == (lane[None, :] // HEAD_DIM)
    g2 = (same.astype(F32) / HEAD_DIM).astype(BF16)
    tile = lambda w: pl.BlockSpec((None, tm, w), lambda bi, i: (bi, i, 0))
    const = lambda shape: pl.BlockSpec(shape, lambda bi, i: (0,) * len(shape), pipeline_mode=pl.Buffered(1))
    tab = pl.BlockSpec((tm, HEAD_W), lambda bi, i: (i, 0))
    in_specs = [
        tile(d),
        pl.BlockSpec((None, HALO_ROWS, d), lambda bi, i: (bi, jnp.maximum(i * hb - 1, 0), 0)),
        pl.BlockSpec((None, HALO_ROWS, d), lambda bi, i: (bi, jnp.minimum((i + 1) * hb, nt * hb - 1), 0)),
        pl.BlockSpec((None, 3, d), lambda bi, i: (bi, 0, 0)),
        const((1, d)),
        const(w_in.shape),
        const(conv_w.shape),
        const((1, HEAD_W)),
        const((1, HEAD_W)),
        const(g2.shape),
        tab, tab, tab,
        const(w_conv_out.shape),
    ]
    out_specs = [tile(D_ATTN), tile(D_ATTN), tile(2 * D_ATTN), tile(D_ATTN), tile(d), tile(d)]
    out_shape = [jax.ShapeDtypeStruct((b, s, w), BF16)
                 for w in (D_ATTN, D_ATTN, 2 * D_ATTN, D_ATTN, d, d)]
    return pl.pallas_call(
        functools.partial(_inproj_kernel, q_scale=HEAD_DIM ** -0.5 * LOG2E),
        grid=(b, nt),
        in_specs=in_specs,
        out_specs=out_specs,
        out_shape=out_shape,
        compiler_params=pltpu.CompilerParams(
            dimension_semantics=("parallel", "parallel"),
            vmem_limit_bytes=V7X_VMEM_LIMIT_BYTES),
        name="inproj",
    )(x, x, x, mod, norm_g.reshape(1, d), w_in, conv_w,
      jnp.tile(q_norm_g, 2).reshape(1, HEAD_W), jnp.tile(k_norm_g, 2).reshape(1, HEAD_W),
      g2, *tables, w_conv_out)


def _attn_kernel(q_ref, k_ref, v_ref, sz_ref, lq1_ref, lk1_ref, lq2_ref, lk2_ref, sg_ref,
                 o_ref, acc_ref, m_ref, qz_ref, s0_ref, s1_ref, *, tk, lam_init):
    tq = q_ref.shape[0]
    nkv = k_ref.shape[0] // tk
    q = q_ref[...]
    lane = lax.broadcasted_iota(jnp.int32, (tq, HEAD_W), 1)
    zero = jnp.zeros_like(q)
    qz_ref[0:tq, :] = jnp.where(lane < HEAD_DIM, q, zero)
    qz_ref[tq:2 * tq, :] = jnp.where(lane >= HEAD_DIM, q, zero)
    m_ref[...] = jnp.full(m_ref.shape, -jnp.inf, F32)
    acc_ref[...] = jnp.zeros(acc_ref.shape, F32)

    def scores(j, s_ref):
        s_ref[...] = lax.dot_general(qz_ref[...], k_ref[j * tk:(j + 1) * tk, :], (((1,), (1,)), ((), ())),
                                     preferred_element_type=F32)

    def accumulate(j, s_ref):
        s = s_ref[...]
        m_prev = m_ref[...]
        m_new = jnp.maximum(m_prev, jnp.max(s, axis=1, keepdims=True))
        alpha = jnp.exp2(m_prev - m_new)
        p = jnp.exp2(s - m_new).astype(BF16)
        acc_ref[...] = alpha * acc_ref[...] + jnp.dot(p, v_ref[j * tk:(j + 1) * tk, :],
                                                      preferred_element_type=F32)
        m_ref[...] = m_new

    s_refs = (s0_ref, s1_ref)
    scores(0, s_refs[0])
    for j in range(nkv):
        if j + 1 < nkv:
            scores(j + 1, s_refs[(j + 1) % 2])
        accumulate(j, s_refs[j % 2])

    acc = acc_ref[...]
    o = acc[:, :HEAD_W] / acc[:, HEAD_W:]
    lam = (jnp.exp(jnp.sum(lq1_ref[...] * lk1_ref[...], axis=1, keepdims=True))
           - jnp.exp(jnp.sum(lq2_ref[...] * lk2_ref[...], axis=1, keepdims=True)) + lam_init)
    od = o[:tq] - lam * o[tq:]
    ms = jnp.mean(od * od, axis=-1, keepdims=True)
    on = od * lax.rsqrt(ms + EPS) * sg_ref[...] * (1.0 - lam_init)
    o_ref[...] = (on * sz_ref[...].astype(F32)).astype(BF16)


def _attn_call(q, k, v_ext, sz, lam_q1, lam_k1, lam_q2, lam_k2, subln_g, *, tq, tk, lam_init):
    b, s, _ = q.shape
    assert s % tq == 0 and s % tk == 0, (s, tq, tk)
    qtile = pl.BlockSpec((None, tq, HEAD_W), lambda bi, h, i: (bi, i, h))
    vec = lambda w: pl.BlockSpec((1, w), lambda bi, h, i: (0, 0))
    return pl.pallas_call(
        functools.partial(_attn_kernel, tk=tk, lam_init=lam_init),
        grid=(b, N_HEADS, s // tq),
        in_specs=[qtile,
                  pl.BlockSpec((None, s, HEAD_W), lambda bi, h, i: (bi, 0, h)),
                  pl.BlockSpec((None, s, 2 * HEAD_W), lambda bi, h, i: (bi, 0, h)),
                  qtile,
                  vec(HEAD_DIM), vec(HEAD_DIM), vec(HEAD_DIM), vec(HEAD_DIM), vec(HEAD_W)],
        out_specs=qtile,
        out_shape=jax.ShapeDtypeStruct((b, s, D_ATTN), BF16),
        scratch_shapes=[pltpu.VMEM((2 * tq, 2 * HEAD_W), F32), pltpu.VMEM((2 * tq, 1), F32),
                        pltpu.VMEM((2 * tq, HEAD_W), BF16),
                        pltpu.VMEM((2 * tq, tk), F32), pltpu.VMEM((2 * tq, tk), F32)],
        compiler_params=pltpu.CompilerParams(
            dimension_semantics=("parallel", "parallel", "parallel"),
            vmem_limit_bytes=V7X_VMEM_LIMIT_BYTES),
        name="diff_attn",
    )(q, k, v_ext, sz, lam_q1.reshape(1, -1), lam_k1.reshape(1, -1), lam_q2.reshape(1, -1),
      lam_k2.reshape(1, -1), subln_g.reshape(1, -1))


def _out_kernel(x_ref, og_ref, a_ref, sgb_ref, mod_ref, wao_ref, wo_ref, y_ref):
    branch_b = jnp.dot(og_ref[...], wao_ref[...], preferred_element_type=F32)
    merged = a_ref[...].astype(F32) + sgb_ref[...].astype(F32) * branch_b
    z = jnp.dot(merged.astype(BF16), wo_ref[...], preferred_element_type=F32)
    y_ref[...] = x_ref[...] + mod_ref[2:3, :] * z


def _out_call(x, og, a, sgb, mod, w_attn_out, w_out, *, tm):
    b, s, d = x.shape
    tile = lambda w: pl.BlockSpec((None, tm, w), lambda bi, i: (bi, i, 0))
    const = lambda shape: pl.BlockSpec(shape, lambda bi, i: (0,) * len(shape))
    return pl.pallas_call(
        _out_kernel,
        grid=(b, s // tm),
        in_specs=[tile(d), tile(D_ATTN), tile(d), tile(d),
                  pl.BlockSpec((None, 3, d), lambda bi, i: (bi, 0, 0)),
                  const(w_attn_out.shape), const(w_out.shape)],
        out_specs=tile(d),
        out_shape=jax.ShapeDtypeStruct((b, s, d), F32),
        compiler_params=pltpu.CompilerParams(
            dimension_semantics=("parallel", "parallel"),
            vmem_limit_bytes=V7X_VMEM_LIMIT_BYTES),
        name="out_proj",
    )(x, og, a, sgb, mod, w_attn_out, w_out)


def _tiles(seq):
    if seq <= 2048:
        return dict(tm=min(seq, 512), tq=min(seq, 512), tk=min(seq, 1024))
    return dict(tm=512, tq=256, tk=2048)


def _layer(x, mod, layer_idx, tables, norm_g, w_in, conv_w, q_norm_g, k_norm_g, lam_q1, lam_k1, lam_q2,
           lam_k2, subln_g, w_conv_out, w_attn_out, w_out):
    t = _tiles(x.shape[1])
    lam_init = 0.8 - 0.6 * math.exp(-0.3 * layer_idx)
    q, k, v_ext, sz, a, sgb = _inproj_call(x, mod, norm_g, w_in, conv_w, q_norm_g, k_norm_g, tables,
                                           w_conv_out, tm=t["tm"])
    og = _attn_call(q, k, v_ext, sz, lam_q1, lam_k1, lam_q2, lam_k2, subln_g,
                    tq=t["tq"], tk=t["tk"], lam_init=lam_init)
    return _out_call(x, og, a, sgb, mod, w_attn_out, w_out, tm=t["tm"])


def kernel(x_prompt, x_sample, c_prompt, c_sample, norm_g, w_ada, b_ada, w_in, conv_w, q_norm_g, k_norm_g,
           lam_q1, lam_k1, lam_q2, lam_k2, subln_g, w_conv_out, w_attn_out, w_out):
    depth = norm_g.shape[0]
    groups = [(x_prompt, c_prompt), (x_sample, c_sample)]
    n_rows = sum(c.shape[0] for _, c in groups)
    pad = -n_rows % 8
    c_all = jnp.concatenate([c for _, c in groups] + [jnp.zeros((pad, D_MODEL), F32)], axis=0)
    tables = _rope_call(max(x.shape[1] for x, _ in groups))
    xs = [x for x, _ in groups]
    for l in range(depth):
        mod_all = _mod_call(c_all, w_ada[l], b_ada[l])
        w_in_b, wco_b, wao_b, wo_b = (w[l].astype(BF16) for w in (w_in, w_conv_out, w_attn_out, w_out))
        row = 0
        for gi, (_, c) in enumerate(groups):
            nb = c.shape[0]
            mod = mod_all[row:row + nb].reshape(nb, 3, D_MODEL)
            row += nb
            xs[gi] = _layer(xs[gi], mod, l, tables, norm_g[l], w_in_b, conv_w[l],
                            q_norm_g[l], k_norm_g[l], lam_q1[l], lam_k1[l], lam_q2[l], lam_k2[l],
                            subln_g[l], wco_b, wao_b, wo_b)
    return tuple(xs)
```

```python
import functools
import math

import jax
import jax.numpy as jnp
from jax import lax
from jax.experimental import pallas as pl
from jax.experimental.pallas import tpu as pltpu

F32 = jnp.float32
BF16 = jnp.bfloat16

D_MODEL = 1024
D_CONV = 512
N_HEADS = 4
HEAD_DIM = 64
HEAD_W = 2 * HEAD_DIM
D_ATTN = N_HEADS * HEAD_W
ROT_DIM = HEAD_DIM // 4
ROT_HALF = ROT_DIM // 2
ROPE_THETA = 500000.0
EPS = 1e-6
LOG2E = 1.4426950408889634

_C_CONV = 0
_C_Q = 4 * D_CONV
_C_K = _C_Q + D_ATTN
_C_V = _C_K + D_ATTN
_C_AZ = _C_V + D_ATTN
_C_GA = _C_AZ + D_ATTN
_C_GB = _C_GA + D_MODEL

V7X_VMEM_LIMIT_BYTES = 56 * 1024 * 1024
HALO_ROWS = 8
MXU_K = 256


def _silu(z):
    return z * jax.nn.sigmoid(z)


def _mod_kernel(c_ref, w_ref, b_ref, o_ref):
    s = _silu(c_ref[...])
    o_ref[...] = jnp.dot(s, w_ref[...], preferred_element_type=F32,
                         precision=lax.Precision.HIGHEST) + b_ref[...]


def _mod_call(c_all, w_ada, b_ada):
    rows, d = c_all.shape
    n = w_ada.shape[1]
    tn = 1024
    return pl.pallas_call(
        _mod_kernel,
        grid=(n // tn,),
        in_specs=[pl.BlockSpec((rows, d), lambda j: (0, 0)),
                  pl.BlockSpec((d, tn), lambda j: (0, j)),
                  pl.BlockSpec((1, tn), lambda j: (0, j))],
        out_specs=pl.BlockSpec((rows, tn), lambda j: (0, j)),
        out_shape=jax.ShapeDtypeStruct((rows, n), F32),
        name="adaln_mod",
    )(c_all, w_ada, b_ada.reshape(1, n))


def _rope_kernel(inv_ref, m1_ref, m2_ref, cos_ref, sa_ref, sb_ref):
    ts = cos_ref.shape[0]
    pos = (pl.program_id(0) * ts + lax.broadcasted_iota(jnp.int32, (ts, HEAD_W), 0)).astype(F32)
    ang = pos * inv_ref[...]
    sin = jnp.sin(ang)
    cos_ref[...] = jnp.cos(ang)
    sa_ref[...] = -sin * m1_ref[...]
    sb_ref[...] = sin * m2_ref[...]


def _rope_call(seq):
    lane = jnp.arange(HEAD_W)
    sub = lane % HEAD_DIM
    inv8 = ROPE_THETA ** (-jnp.arange(ROT_HALF, dtype=F32) / ROT_HALF)
    inv = jnp.where(sub < ROT_DIM, inv8[sub % ROT_HALF], 0.0).astype(F32).reshape(1, HEAD_W)
    m1 = (sub < ROT_HALF).astype(F32).reshape(1, HEAD_W)
    m2 = ((sub >= ROT_HALF) & (sub < ROT_DIM)).astype(F32).reshape(1, HEAD_W)
    ts = min(seq, 2048)
    vec = pl.BlockSpec((1, HEAD_W), lambda i: (0, 0))
    tab = pl.BlockSpec((ts, HEAD_W), lambda i: (i, 0))
    return pl.pallas_call(
        _rope_kernel,
        grid=(seq // ts,),
        in_specs=[vec, vec, vec],
        out_specs=[tab, tab, tab],
        out_shape=[jax.ShapeDtypeStruct((seq, HEAD_W), F32)] * 3,
        name="rope_tables",
    )(inv, m1, m2)


def _group_mean_sq(x):
    sq = x * x
    first = lax.broadcasted_iota(jnp.int32, x.shape, 1) < HEAD_DIM
    s_first = jnp.sum(jnp.where(first, sq, 0.0), axis=1, keepdims=True)
    s_second = jnp.sum(jnp.where(first, 0.0, sq), axis=1, keepdims=True)
    return jnp.where(first, s_first, s_second) * (1.0 / HEAD_DIM)


def _qk_norm_rope(x, gain, cos, sa, sb):
    xn = x * lax.rsqrt(_group_mean_sq(x) + EPS) * gain
    return xn * cos + pltpu.roll(xn, HEAD_W - ROT_HALF, 1) * sa + pltpu.roll(xn, ROT_HALF, 1) * sb


def _inproj_kernel(x_ref, xp_ref, xn_ref, mod_ref, ng_ref, win_ref, convw_ref, qg_ref, kg_ref,
                   cos_ref, sa_ref, sb_ref, wco_ref,
                   q_ref, k_ref, v_ref, sz_ref, a_ref, sgb_ref, *, q_scale):
    i = pl.program_id(1)
    last = pl.num_programs(1) - 1
    tm = x_ref.shape[0]
    shift = mod_ref[0:1, :]
    scale1 = 1.0 + mod_ref[1:2, :]
    ng = ng_ref[...]

    def norm_mod(x):
        ms = jnp.mean(x * x, axis=-1, keepdims=True)
        return (x * lax.rsqrt(ms + EPS) * ng) * scale1 + shift

    h = norm_mod(x_ref[...]).astype(BF16)

    def proj(lo, width):
        return jnp.dot(h, win_ref[:, lo:lo + width], preferred_element_type=F32)

    pqk = proj(_C_Q, 2 * D_ATTN)
    cos, sa, sb = cos_ref[...], sa_ref[...], sb_ref[...]
    qg, kg = qg_ref[...], kg_ref[...]
    for hd in range(N_HEADS):
        lo = hd * HEAD_W
        qh = _qk_norm_rope(pqk[:, lo:lo + HEAD_W], qg, cos, sa, sb)
        q_ref[:, lo:lo + HEAD_W] = (qh * q_scale).astype(BF16)
        kh = _qk_norm_rope(pqk[:, D_ATTN + lo:D_ATTN + lo + HEAD_W], kg, cos, sa, sb)
        k_ref[:, lo:lo + HEAD_W] = kh.astype(BF16)

    pc = proj(_C_CONV, 4 * D_CONV)
    cb, cc, cx, cz = (pc[:, j * D_CONV:(j + 1) * D_CONV] for j in range(4))
    u = cc * cx
    hh = norm_mod(jnp.concatenate([xp_ref[...], xn_ref[...]], axis=0)).astype(BF16)
    ph = jnp.dot(hh, win_ref[:, D_CONV:3 * D_CONV], preferred_element_type=F32)
    uh = ph[:, :D_CONV] * ph[:, D_CONV:]
    u_prev = jnp.where(i > 0, uh[HALO_ROWS - 1:HALO_ROWS, :], 0.0)
    u_next = jnp.where(i < last, uh[HALO_ROWS:HALO_ROWS + 1, :], 0.0)
    row = lax.broadcasted_iota(jnp.int32, (tm, D_CONV), 0)
    u_dn = jnp.where(row == 0, u_prev, pltpu.roll(u, 1, 0))
    u_up = jnp.where(row == tm - 1, u_next, pltpu.roll(u, tm - 1, 0))
    cw = convw_ref[...]
    y = cb * (cw[0:1, :] * u_dn + cw[1:2, :] * u + cw[2:3, :] * u_up) * _silu(cz)
    y = y.astype(BF16)

    pv = proj(_C_V, 2 * D_ATTN)
    sz_ref[...] = _silu(pv[:, D_ATTN:]).astype(BF16)
    ones = jnp.ones((tm, HEAD_W), BF16)
    for hd in range(N_HEADS):
        v_ref[:, 2 * hd * HEAD_W:(2 * hd + 1) * HEAD_W] = pv[:, hd * HEAD_W:(hd + 1) * HEAD_W].astype(BF16)
        v_ref[:, (2 * hd + 1) * HEAD_W:(2 * hd + 2) * HEAD_W] = ones

    sga = jax.nn.sigmoid(proj(_C_GA, D_MODEL))
    branch_a = jnp.dot(y, wco_ref[...], preferred_element_type=F32)
    a_ref[...] = (sga * branch_a).astype(BF16)
    sgb_ref[...] = jax.nn.sigmoid(proj(_C_GB, D_MODEL)).astype(BF16)


def _inproj_call(x, mod, norm_g, w_in, conv_w, q_norm_g, k_norm_g, tables, w_conv_out, *, tm):
    b, s, d = x.shape
    nt = s // tm
    hb = tm // HALO_ROWS
    tile = lambda w: pl.BlockSpec((None, tm, w), lambda bi, i: (bi, i, 0))
    const = lambda shape: pl.BlockSpec(shape, lambda bi, i: (0,) * len(shape), pipeline_mode=pl.Buffered(1))
    tab = pl.BlockSpec((tm, HEAD_W), lambda bi, i: (i, 0))
    in_specs = [
        tile(d),
        pl.BlockSpec((None, HALO_ROWS, d), lambda bi, i: (bi, jnp.maximum(i * hb - 1, 0), 0)),
        pl.BlockSpec((None, HALO_ROWS, d), lambda bi, i: (bi, jnp.minimum((i + 1) * hb, nt * hb - 1), 0)),
        pl.BlockSpec((None, 3, d), lambda bi, i: (bi, 0, 0)),
        const((1, d)),
        const(w_in.shape),
        const(conv_w.shape),
        const((1, HEAD_W)),
        const((1, HEAD_W)),
        tab, tab, tab,
        const(w_conv_out.shape),
    ]
    out_specs = [tile(D_ATTN), tile(D_ATTN), tile(2 * D_ATTN), tile(D_ATTN), tile(d), tile(d)]
    out_shape = [jax.ShapeDtypeStruct((b, s, w), BF16)
                 for w in (D_ATTN, D_ATTN, 2 * D_ATTN, D_ATTN, d, d)]
    return pl.pallas_call(
        functools.partial(_inproj_kernel, q_scale=HEAD_DIM ** -0.5 * LOG2E),
        grid=(b, nt),
        in_specs=in_specs,
        out_specs=out_specs,
        out_shape=out_shape,
        compiler_params=pltpu.CompilerParams(
            dimension_semantics=("parallel", "parallel"),
            vmem_limit_bytes=V7X_VMEM_LIMIT_BYTES),
        name="inproj",
    )(x, x, x, mod, norm_g.reshape(1, d), w_in, conv_w,
      jnp.tile(q_norm_g, 2).reshape(1, HEAD_W), jnp.tile(k_norm_g, 2).reshape(1, HEAD_W),
      *tables, w_conv_out)


def _attn_kernel(q_ref, k_ref, v_ref, sz_ref, lq1_ref, lk1_ref, lq2_ref, lk2_ref, sg_ref,
                 o_ref, acc_ref, m_ref, qz_ref, *, tk, lam_init):
    tq = q_ref.shape[0]
    nkv = k_ref.shape[0] // tk
    q = q_ref[...]
    lane = lax.broadcasted_iota(jnp.int32, (tq, HEAD_W), 1)
    zero = jnp.zeros_like(q)
    qz_ref[0:tq, :] = jnp.where(lane < HEAD_DIM, q, zero)
    qz_ref[tq:2 * tq, :] = jnp.where(lane >= HEAD_DIM, q, zero)
    m_ref[...] = jnp.full(m_ref.shape, -jnp.inf, F32)
    acc_ref[...] = jnp.zeros(acc_ref.shape, F32)

    def scores(j):
        return lax.dot_general(qz_ref[...], k_ref[j * tk:(j + 1) * tk, :], (((1,), (1,)), ((), ())),
                               preferred_element_type=F32)

    def accumulate(j, s):
        m_prev = m_ref[...]
        m_new = jnp.maximum(m_prev, jnp.max(s, axis=1, keepdims=True))
        alpha = jnp.exp2(m_prev - m_new)
        p = jnp.exp2(s - m_new).astype(BF16)
        acc_ref[...] = alpha * acc_ref[...] + jnp.dot(p, v_ref[j * tk:(j + 1) * tk, :],
                                                      preferred_element_type=F32)
        m_ref[...] = m_new

    s_next = scores(0)
    for j in range(nkv):
        s_cur = s_next
        if j + 1 < nkv:
            s_next = scores(j + 1)
        accumulate(j, s_cur)

    acc = acc_ref[...]
    o = acc[:, :HEAD_W] / acc[:, HEAD_W:]
    lam = (jnp.exp(jnp.sum(lq1_ref[...] * lk1_ref[...], axis=1, keepdims=True))
           - jnp.exp(jnp.sum(lq2_ref[...] * lk2_ref[...], axis=1, keepdims=True)) + lam_init)
    od = o[:tq] - lam * o[tq:]
    ms = jnp.mean(od * od, axis=-1, keepdims=True)
    on = od * lax.rsqrt(ms + EPS) * sg_ref[...] * (1.0 - lam_init)
    o_ref[...] = (on * sz_ref[...].astype(F32)).astype(BF16)


def _attn_call(q, k, v_ext, sz, lam_q1, lam_k1, lam_q2, lam_k2, subln_g, *, tq, tk, lam_init):
    b, s, _ = q.shape
    assert s % tq == 0 and s % tk == 0, (s, tq, tk)
    qtile = pl.BlockSpec((None, tq, HEAD_W), lambda bi, h, i: (bi, i, h))
    vec = lambda w: pl.BlockSpec((1, w), lambda bi, h, i: (0, 0))
    return pl.pallas_call(
        functools.partial(_attn_kernel, tk=tk, lam_init=lam_init),
        grid=(b, N_HEADS, s // tq),
        in_specs=[qtile,
                  pl.BlockSpec((None, s, HEAD_W), lambda bi, h, i: (bi, 0, h)),
                  pl.BlockSpec((None, s, 2 * HEAD_W), lambda bi, h, i: (bi, 0, h)),
                  qtile,
                  vec(HEAD_DIM), vec(HEAD_DIM), vec(HEAD_DIM), vec(HEAD_DIM), vec(HEAD_W)],
        out_specs=qtile,
        out_shape=jax.ShapeDtypeStruct((b, s, D_ATTN), BF16),
        scratch_shapes=[pltpu.VMEM((2 * tq, 2 * HEAD_W), F32), pltpu.VMEM((2 * tq, 1), F32),
                        pltpu.VMEM((2 * tq, HEAD_W), BF16)],
        compiler_params=pltpu.CompilerParams(
            dimension_semantics=("parallel", "parallel", "parallel"),
            vmem_limit_bytes=V7X_VMEM_LIMIT_BYTES),
        name="diff_attn",
    )(q, k, v_ext, sz, lam_q1.reshape(1, -1), lam_k1.reshape(1, -1), lam_q2.reshape(1, -1),
      lam_k2.reshape(1, -1), subln_g.reshape(1, -1))


def _out_kernel(x_ref, og_ref, a_ref, sgb_ref, mod_ref, wao_ref, wo_ref, y_ref):
    branch_b = jnp.dot(og_ref[...], wao_ref[...], preferred_element_type=F32)
    merged = a_ref[...].astype(F32) + sgb_ref[...].astype(F32) * branch_b
    z = jnp.dot(merged.astype(BF16), wo_ref[...], preferred_element_type=F32)
    y_ref[...] = x_ref[...] + mod_ref[2:3, :] * z


def _out_call(x, og, a, sgb, mod, w_attn_out, w_out, *, tm):
    b, s, d = x.shape
    tile = lambda w: pl.BlockSpec((None, tm, w), lambda bi, i: (bi, i, 0))
    const = lambda shape: pl.BlockSpec(shape, lambda bi, i: (0,) * len(shape))
    return pl.pallas_call(
        _out_kernel,
        grid=(b, s // tm),
        in_specs=[tile(d), tile(D_ATTN), tile(d), tile(d),
                  pl.BlockSpec((None, 3, d), lambda bi, i: (bi, 0, 0)),
                  const(w_attn_out.shape), const(w_out.shape)],
        out_specs=tile(d),
        out_shape=jax.ShapeDtypeStruct((b, s, d), F32),
        compiler_params=pltpu.CompilerParams(
            dimension_semantics=("parallel", "parallel"),
            vmem_limit_bytes=V7X_VMEM_LIMIT_BYTES),
        name="out_proj",
    )(x, og, a, sgb, mod, w_attn_out, w_out)


def _tiles(seq):
    if seq <= 2048:
        return dict(tm=min(seq, 512), tq=min(seq, 512), tk=min(seq, 1024))
    return dict(tm=512, tq=256, tk=4096)


def _layer(x, mod, layer_idx, tables, norm_g, w_in, conv_w, q_norm_g, k_norm_g, lam_q1, lam_k1, lam_q2,
           lam_k2, subln_g, w_conv_out, w_attn_out, w_out):
    t = _tiles(x.shape[1])
    lam_init = 0.8 - 0.6 * math.exp(-0.3 * layer_idx)
    q, k, v_ext, sz, a, sgb = _inproj_call(x, mod, norm_g, w_in, conv_w, q_norm_g, k_norm_g, tables,
                                           w_conv_out, tm=t["tm"])
    og = _attn_call(q, k, v_ext, sz, lam_q1, lam_k1, lam_q2, lam_k2, subln_g,
                    tq=t["tq"], tk=t["tk"], lam_init=lam_init)
    return _out_call(x, og, a, sgb, mod, w_attn_out, w_out, tm=t["tm"])


def kernel(x_prompt, x_sample, c_prompt, c_sample, norm_g, w_ada, b_ada, w_in, conv_w, q_norm_g, k_norm_g,
           lam_q1, lam_k1, lam_q2, lam_k2, subln_g, w_conv_out, w_attn_out, w_out):
    depth = norm_g.shape[0]
    groups = [(x_prompt, c_prompt), (x_sample, c_sample)]
    n_rows = sum(c.shape[0] for _, c in groups)
    pad = -n_rows % 8
    c_all = jnp.concatenate([c for _, c in groups] + [jnp.zeros((pad, D_MODEL), F32)], axis=0)
    tables = _rope_call(max(x.shape[1] for x, _ in groups))
    xs = [x for x, _ in groups]
    for l in range(depth):
        mod_all = _mod_call(c_all, w_ada[l], b_ada[l])
        w_in_b, wco_b, wao_b, wo_b = (w[l].astype(BF16) for w in (w_in, w_conv_out, w_attn_out, w_out))
        row = 0
        for gi, (_, c) in enumerate(groups):
            nb = c.shape[0]
            mod = mod_all[row:row + nb].reshape(nb, 3, D_MODEL)
            row += nb
            xs[gi] = _layer(xs[gi], mod, l, tables, norm_g[l], w_in_b, conv_w[l],
                            q_norm_g[l], k_norm_g[l], lam_q1[l], lam_k1[l], lam_q2[l], lam_k2[l],
                            subln_g[l], wco_b, wao_b, wo_b)
    return tuple(xs)
```

```python
import functools
import math

import jax
import jax.numpy as jnp
from jax import lax
from jax.experimental import pallas as pl
from jax.experimental.pallas import tpu as pltpu

F32 = jnp.float32
BF16 = jnp.bfloat16

D_MODEL = 1024
D_CONV = 512
N_HEADS = 4
HEAD_DIM = 64
HEAD_W = 2 * HEAD_DIM
D_ATTN = N_HEADS * HEAD_W
ROT_DIM = HEAD_DIM // 4
ROT_HALF = ROT_DIM // 2
ROPE_THETA = 500000.0
EPS = 1e-6
LOG2E = 1.4426950408889634
Q_SCALE = HEAD_DIM ** -0.5 * LOG2E

_C_CONV = 0
_C_Q = 4 * D_CONV
_C_K = _C_Q + D_ATTN
_C_V = _C_K + D_ATTN
_C_AZ = _C_V + D_ATTN
_C_GA = _C_AZ + D_ATTN
_C_GB = _C_GA + D_MODEL

V7X_VMEM_LIMIT_BYTES = 56 * 1024 * 1024
HALO_ROWS = 8
MAX_FIXED_SHIFT = 48.0


def _silu(z):
    return z * jax.nn.sigmoid(z)


def _mod_kernel(c_ref, w_ref, b_ref, o_ref):
    s = _silu(c_ref[...])
    o_ref[...] = jnp.dot(s, w_ref[...], preferred_element_type=F32,
                         precision=lax.Precision.HIGHEST) + b_ref[...]


def _mod_call(c_all, w_ada, b_ada):
    rows, d = c_all.shape
    n = w_ada.shape[1]
    tn = 1024
    return pl.pallas_call(
        _mod_kernel,
        grid=(n // tn,),
        in_specs=[pl.BlockSpec((rows, d), lambda j: (0, 0)),
                  pl.BlockSpec((d, tn), lambda j: (0, j)),
                  pl.BlockSpec((1, tn), lambda j: (0, j))],
        out_specs=pl.BlockSpec((rows, tn), lambda j: (0, j)),
        out_shape=jax.ShapeDtypeStruct((rows, n), F32),
        name="adaln_mod",
    )(c_all, w_ada, b_ada.reshape(1, n))


def _rope_kernel(inv_ref, m1_ref, m2_ref, cos_ref, sa_ref, sb_ref):
    ts = cos_ref.shape[0]
    pos = (pl.program_id(0) * ts + lax.broadcasted_iota(jnp.int32, (ts, HEAD_W), 0)).astype(F32)
    ang = pos * inv_ref[...]
    sin = jnp.sin(ang)
    cos_ref[...] = jnp.cos(ang)
    sa_ref[...] = -sin * m1_ref[...]
    sb_ref[...] = sin * m2_ref[...]


def _rope_call(seq):
    lane = jnp.arange(HEAD_W)
    sub = lane % HEAD_DIM
    inv8 = ROPE_THETA ** (-jnp.arange(ROT_HALF, dtype=F32) / ROT_HALF)
    inv = jnp.where(sub < ROT_DIM, inv8[sub % ROT_HALF], 0.0).astype(F32).reshape(1, HEAD_W)
    m1 = (sub < ROT_HALF).astype(F32).reshape(1, HEAD_W)
    m2 = ((sub >= ROT_HALF) & (sub < ROT_DIM)).astype(F32).reshape(1, HEAD_W)
    ts = min(seq, 2048)
    vec = pl.BlockSpec((1, HEAD_W), lambda i: (0, 0))
    tab = pl.BlockSpec((ts, HEAD_W), lambda i: (i, 0))
    return pl.pallas_call(
        _rope_kernel,
        grid=(seq // ts,),
        in_specs=[vec, vec, vec],
        out_specs=[tab, tab, tab],
        out_shape=[jax.ShapeDtypeStruct((seq, HEAD_W), F32)] * 3,
        name="rope_tables",
    )(inv, m1, m2)


def _group_mean_sq(x):
    sq = x * x
    first = lax.broadcasted_iota(jnp.int32, x.shape, 1) < HEAD_DIM
    s_first = jnp.sum(jnp.where(first, sq, 0.0), axis=1, keepdims=True)
    s_second = jnp.sum(jnp.where(first, 0.0, sq), axis=1, keepdims=True)
    return jnp.where(first, s_first, s_second) * (1.0 / HEAD_DIM)


def _qk_norm_rope(x, gain, cos, sa, sb):
    xn = x * lax.rsqrt(_group_mean_sq(x) + EPS) * gain
    return xn * cos + pltpu.roll(xn, HEAD_W - ROT_HALF, 1) * sa + pltpu.roll(xn, ROT_HALF, 1) * sb


def _inproj_kernel(x_ref, xp_ref, xn_ref, mod_ref, ng_ref, win_ref, convw_ref, qg_ref, kg_ref,
                   cos_ref, sa_ref, sb_ref, wco_ref,
                   q_ref, k_ref, v_ref, sz_ref, a_ref, sgb_ref, *, q_scale):
    i = pl.program_id(1)
    last = pl.num_programs(1) - 1
    tm = x_ref.shape[0]
    shift = mod_ref[0:1, :]
    scale1 = 1.0 + mod_ref[1:2, :]
    ng = ng_ref[...]

    def norm_mod(x):
        ms = jnp.mean(x * x, axis=-1, keepdims=True)
        return (x * lax.rsqrt(ms + EPS) * ng) * scale1 + shift

    h = norm_mod(x_ref[...]).astype(BF16)

    def proj(lo, width):
        return jnp.dot(h, win_ref[:, lo:lo + width], preferred_element_type=F32)

    pqk = proj(_C_Q, 2 * D_ATTN)
    cos, sa, sb = cos_ref[...], sa_ref[...], sb_ref[...]
    qg, kg = qg_ref[...], kg_ref[...]
    for hd in range(N_HEADS):
        lo = hd * HEAD_W
        qh = _qk_norm_rope(pqk[:, lo:lo + HEAD_W], qg, cos, sa, sb)
        q_ref[:, lo:lo + HEAD_W] = (qh * q_scale).astype(BF16)
        kh = _qk_norm_rope(pqk[:, D_ATTN + lo:D_ATTN + lo + HEAD_W], kg, cos, sa, sb)
        k_ref[:, lo:lo + HEAD_W] = kh.astype(BF16)

    hh = norm_mod(jnp.concatenate([xp_ref[...], xn_ref[...]], axis=0)).astype(BF16)
    pc = jnp.dot(jnp.concatenate([h, hh], axis=0), win_ref[:, _C_CONV:_C_CONV + 4 * D_CONV],
                 preferred_element_type=F32)
    cb, cc, cx, cz = (pc[:tm, j * D_CONV:(j + 1) * D_CONV] for j in range(4))
    u = cc * cx
    uh = pc[tm:, D_CONV:2 * D_CONV] * pc[tm:, 2 * D_CONV:3 * D_CONV]
    u_prev = jnp.where(i > 0, uh[HALO_ROWS - 1:HALO_ROWS, :], 0.0)
    u_next = jnp.where(i < last, uh[HALO_ROWS:HALO_ROWS + 1, :], 0.0)
    row = lax.broadcasted_iota(jnp.int32, (tm, D_CONV), 0)
    u_dn = jnp.where(row == 0, u_prev, pltpu.roll(u, 1, 0))
    u_up = jnp.where(row == tm - 1, u_next, pltpu.roll(u, tm - 1, 0))
    cw = convw_ref[...]
    y = cb * (cw[0:1, :] * u_dn + cw[1:2, :] * u + cw[2:3, :] * u_up) * _silu(cz)
    y = y.astype(BF16)

    pv = proj(_C_V, 2 * D_ATTN)
    sz_ref[...] = _silu(pv[:, D_ATTN:]).astype(BF16)
    ones = jnp.ones((tm, HEAD_W), BF16)
    for hd in range(N_HEADS):
        v_ref[:, 2 * hd * HEAD_W:(2 * hd + 1) * HEAD_W] = pv[:, hd * HEAD_W:(hd + 1) * HEAD_W].astype(BF16)
        v_ref[:, (2 * hd + 1) * HEAD_W:(2 * hd + 2) * HEAD_W] = ones

    sga = jax.nn.sigmoid(proj(_C_GA, D_MODEL))
    branch_a = jnp.dot(y, wco_ref[...], preferred_element_type=F32)
    a_ref[...] = (sga * branch_a).astype(BF16)
    sgb_ref[...] = jax.nn.sigmoid(proj(_C_GB, D_MODEL)).astype(BF16)


def _inproj_call(x, mod, norm_g, w_in, conv_w, q_norm_g, k_norm_g, tables, w_conv_out, *, tm):
    b, s, d = x.shape
    nt = s // tm
    hb = tm // HALO_ROWS
    tile = lambda w: pl.BlockSpec((None, tm, w), lambda bi, i: (bi, i, 0))
    const = lambda shape: pl.BlockSpec(shape, lambda bi, i: (0,) * len(shape), pipeline_mode=pl.Buffered(1))
    tab = pl.BlockSpec((tm, HEAD_W), lambda bi, i: (i, 0))
    in_specs = [
        tile(d),
        pl.BlockSpec((None, HALO_ROWS, d), lambda bi, i: (bi, jnp.maximum(i * hb - 1, 0), 0)),
        pl.BlockSpec((None, HALO_ROWS, d), lambda bi, i: (bi, jnp.minimum((i + 1) * hb, nt * hb - 1), 0)),
        pl.BlockSpec((None, 3, d), lambda bi, i: (bi, 0, 0)),
        const((1, d)),
        const(w_in.shape),
        const(conv_w.shape),
        const((1, HEAD_W)),
        const((1, HEAD_W)),
        tab, tab, tab,
        const(w_conv_out.shape),
    ]
    out_specs = [tile(D_ATTN), tile(D_ATTN), tile(2 * D_ATTN), tile(D_ATTN), tile(d), tile(d)]
    out_shape = [jax.ShapeDtypeStruct((b, s, w), BF16)
                 for w in (D_ATTN, D_ATTN, 2 * D_ATTN, D_ATTN, d, d)]
    return pl.pallas_call(
        functools.partial(_inproj_kernel, q_scale=Q_SCALE),
        grid=(b, nt),
        in_specs=in_specs,
        out_specs=out_specs,
        out_shape=out_shape,
        compiler_params=pltpu.CompilerParams(
            dimension_semantics=("parallel", "parallel"),
            vmem_limit_bytes=V7X_VMEM_LIMIT_BYTES),
        name="inproj",
    )(x, x, x, mod, norm_g.reshape(1, d), w_in, conv_w,
      jnp.tile(q_norm_g, 2).reshape(1, HEAD_W), jnp.tile(k_norm_g, 2).reshape(1, HEAD_W),
      *tables, w_conv_out)


def _attn_kernel(bound_ref, q_ref, k_ref, v_ref, sz_ref, lq1_ref, lk1_ref, lq2_ref, lk2_ref, sg_ref,
                 o_ref, acc_ref, m_ref, qz_ref, *, tk, tk_fast, lam_init):
    tq = q_ref.shape[0]
    seq = k_ref.shape[0]
    q = q_ref[...]
    first = lax.broadcasted_iota(jnp.int32, (tq, HEAD_W), 1) < HEAD_DIM
    zero = jnp.zeros_like(q)
    qz_ref[0:tq, :] = jnp.where(first, q, zero)
    qz_ref[tq:2 * tq, :] = jnp.where(first, zero, q)

    bound = bound_ref[0]
    fast = bound <= MAX_FIXED_SHIFT

    def scores(j, width):
        return lax.dot_general(qz_ref[...], k_ref[j * width:(j + 1) * width, :], (((1,), (1,)), ((), ())),
                               preferred_element_type=F32)

    @pl.when(fast)
    def _():
        n = seq // tk_fast
        acc = None
        s_next = scores(0, tk_fast)
        for j in range(n):
            s_cur = s_next
            if j + 1 < n:
                s_next = scores(j + 1, tk_fast)
            p = jnp.exp2(s_cur - bound).astype(BF16)
            d = jnp.dot(p, v_ref[j * tk_fast:(j + 1) * tk_fast, :], preferred_element_type=F32)
            acc = d if acc is None else acc + d
        acc_ref[...] = acc

    @pl.when(jnp.logical_not(fast))
    def _():
        n = seq // tk
        m_ref[...] = jnp.full(m_ref.shape, -jnp.inf, F32)
        acc_ref[...] = jnp.zeros(acc_ref.shape, F32)

        def accumulate(j, s):
            m_prev = m_ref[...]
            m_new = jnp.maximum(m_prev, jnp.max(s, axis=1, keepdims=True))
            alpha = jnp.exp2(m_prev - m_new)
            p = jnp.exp2(s - m_new).astype(BF16)
            acc_ref[...] = alpha * acc_ref[...] + jnp.dot(p, v_ref[j * tk:(j + 1) * tk, :],
                                                          preferred_element_type=F32)
            m_ref[...] = m_new

        s_next = scores(0, tk)
        for j in range(n):
            s_cur = s_next
            if j + 1 < n:
                s_next = scores(j + 1, tk)
            accumulate(j, s_cur)

    acc = acc_ref[...]
    o = acc[:, :HEAD_W] / acc[:, HEAD_W:]
    lam = (jnp.exp(jnp.sum(lq1_ref[...] * lk1_ref[...], axis=1, keepdims=True))
           - jnp.exp(jnp.sum(lq2_ref[...] * lk2_ref[...], axis=1, keepdims=True)) + lam_init)
    od = o[:tq] - lam * o[tq:]
    ms = jnp.mean(od * od, axis=-1, keepdims=True)
    on = od * lax.rsqrt(ms + EPS) * sg_ref[...] * (1.0 - lam_init)
    o_ref[...] = (on * sz_ref[...].astype(F32)).astype(BF16)


def _attn_call(score_bound, q, k, v_ext, sz, lam_q1, lam_k1, lam_q2, lam_k2, subln_g, *, tq, tk, tk_fast,
               lam_init):
    b, s, _ = q.shape
    assert s % tq == 0 and s % tk == 0 and s % tk_fast == 0, (s, tq, tk, tk_fast)
    qtile = pl.BlockSpec((None, tq, HEAD_W), lambda bi, h, i: (bi, i, h))
    vec = lambda w: pl.BlockSpec((1, w), lambda bi, h, i: (0, 0))
    return pl.pallas_call(
        functools.partial(_attn_kernel, tk=tk, tk_fast=tk_fast, lam_init=lam_init),
        grid=(b, N_HEADS, s // tq),
        in_specs=[pl.BlockSpec(memory_space=pltpu.SMEM),
                  qtile,
                  pl.BlockSpec((None, s, HEAD_W), lambda bi, h, i: (bi, 0, h)),
                  pl.BlockSpec((None, s, 2 * HEAD_W), lambda bi, h, i: (bi, 0, h)),
                  qtile,
                  vec(HEAD_DIM), vec(HEAD_DIM), vec(HEAD_DIM), vec(HEAD_DIM), vec(HEAD_W)],
        out_specs=qtile,
        out_shape=jax.ShapeDtypeStruct((b, s, D_ATTN), BF16),
        scratch_shapes=[pltpu.VMEM((2 * tq, 2 * HEAD_W), F32), pltpu.VMEM((2 * tq, 1), F32),
                        pltpu.VMEM((2 * tq, HEAD_W), BF16)],
        compiler_params=pltpu.CompilerParams(
            dimension_semantics=("parallel", "parallel", "parallel"),
            vmem_limit_bytes=V7X_VMEM_LIMIT_BYTES),
        name="diff_attn",
    )(score_bound, q, k, v_ext, sz, lam_q1.reshape(1, -1), lam_k1.reshape(1, -1), lam_q2.reshape(1, -1),
      lam_k2.reshape(1, -1), subln_g.reshape(1, -1))


def _out_kernel(x_ref, og_ref, a_ref, sgb_ref, mod_ref, wao_ref, wo_ref, y_ref):
    branch_b = jnp.dot(og_ref[...], wao_ref[...], preferred_element_type=F32)
    merged = a_ref[...].astype(F32) + sgb_ref[...].astype(F32) * branch_b
    z = jnp.dot(merged.astype(BF16), wo_ref[...], preferred_element_type=F32)
    y_ref[...] = x_ref[...] + mod_ref[2:3, :] * z


def _out_call(x, og, a, sgb, mod, w_attn_out, w_out, *, tm):
    b, s, d = x.shape
    tile = lambda w: pl.BlockSpec((None, tm, w), lambda bi, i: (bi, i, 0))
    const = lambda shape: pl.BlockSpec(shape, lambda bi, i: (0,) * len(shape))
    return pl.pallas_call(
        _out_kernel,
        grid=(b, s // tm),
        in_specs=[tile(d), tile(D_ATTN), tile(d), tile(d),
                  pl.BlockSpec((None, 3, d), lambda bi, i: (bi, 0, 0)),
                  const(w_attn_out.shape), const(w_out.shape)],
        out_specs=tile(d),
        out_shape=jax.ShapeDtypeStruct((b, s, d), F32),
        compiler_params=pltpu.CompilerParams(
            dimension_semantics=("parallel", "parallel"),
            vmem_limit_bytes=V7X_VMEM_LIMIT_BYTES),
        name="out_proj",
    )(x, og, a, sgb, mod, w_attn_out, w_out)


def _tiles(seq):
    if seq <= 2048:
        return dict(tm=min(seq, 512), to=min(seq, 1024), tq=min(seq, 512), tk=min(seq, 1024), tk_fast=min(seq, 256))
    return dict(tm=512, to=1024, tq=256, tk=4096, tk_fast=256)


def _layer(x, mod, layer_idx, tables, norm_g, w_in, conv_w, q_norm_g, k_norm_g, lam_q1, lam_k1, lam_q2,
           lam_k2, subln_g, w_conv_out, w_attn_out, w_out):
    t = _tiles(x.shape[1])
    lam_init = 0.8 - 0.6 * math.exp(-0.3 * layer_idx)
    q, k, v_ext, sz, a, sgb = _inproj_call(x, mod, norm_g, w_in, conv_w, q_norm_g, k_norm_g, tables,
                                           w_conv_out, tm=t["tm"])
    score_bound = (HEAD_DIM * Q_SCALE * jnp.max(jnp.abs(q_norm_g)) * jnp.max(jnp.abs(k_norm_g))).reshape(1)
    og = _attn_call(score_bound.astype(F32), q, k, v_ext, sz, lam_q1, lam_k1, lam_q2, lam_k2, subln_g,
                    tq=t["tq"], tk=t["tk"], tk_fast=t["tk_fast"], lam_init=lam_init)
    return _out_call(x, og, a, sgb, mod, w_attn_out, w_out, tm=t["to"])


def kernel(x_prompt, x_sample, c_prompt, c_sample, norm_g, w_ada, b_ada, w_in, conv_w, q_norm_g, k_norm_g,
           lam_q1, lam_k1, lam_q2, lam_k2, subln_g, w_conv_out, w_attn_out, w_out):
    depth = norm_g.shape[0]
    groups = [(x_prompt, c_prompt), (x_sample, c_sample)]
    n_rows = sum(c.shape[0] for _, c in groups)
    pad = -n_rows % 8
    c_all = jnp.concatenate([c for _, c in groups] + [jnp.zeros((pad, D_MODEL), F32)], axis=0)
    tables = _rope_call(max(x.shape[1] for x, _ in groups))
    xs = [x for x, _ in groups]
    for l in range(depth):
        mod_all = _mod_call(c_all, w_ada[l], b_ada[l])
        w_in_b, wco_b, wao_b, wo_b = (w[l].astype(BF16) for w in (w_in, w_conv_out, w_attn_out, w_out))
        row = 0
        for gi, (_, c) in enumerate(groups):
            nb = c.shape[0]
            mod = mod_all[row:row + nb].reshape(nb, 3, D_MODEL)
            row += nb
            xs[gi] = _layer(xs[gi], mod, l, tables, norm_g[l], w_in_b, conv_w[l],
                            q_norm_g[l], k_norm_g[l], lam_q1[l], lam_k1[l], lam_q2[l], lam_k2[l],
                            subln_g[l], wco_b, wao_b, wo_b)
    return tuple(xs)
```

```python
import functools
import math

import jax
import jax.numpy as jnp
from jax import lax
from jax.experimental import pallas as pl
from jax.experimental.pallas import tpu as pltpu

F32 = jnp.float32
BF16 = jnp.bfloat16

D_MODEL = 1024
D_CONV = 512
N_HEADS = 4
HEAD_DIM = 64
HEAD_W = 2 * HEAD_DIM
D_ATTN = N_HEADS * HEAD_W
ROT_DIM = HEAD_DIM // 4
ROT_HALF = ROT_DIM // 2
ROPE_THETA = 500000.0
EPS = 1e-6
LOG2E = 1.4426950408889634
Q_SCALE = HEAD_DIM ** -0.5 * LOG2E

_C_CONV = 0
_C_Q = 4 * D_CONV
_C_K = _C_Q + D_ATTN
_C_V = _C_K + D_ATTN
_C_AZ = _C_V + D_ATTN
_C_GA = _C_AZ + D_ATTN
_C_GB = _C_GA + D_MODEL

V7X_VMEM_LIMIT_BYTES = 56 * 1024 * 1024
HALO_ROWS = 8
MAX_FIXED_SHIFT = 48.0


def _silu(z):
    return z * jax.nn.sigmoid(z)


def _mod_kernel(c_ref, w_ref, b_ref, o_ref):
    s = _silu(c_ref[...])
    o_ref[...] = jnp.dot(s, w_ref[...], preferred_element_type=F32,
                         precision=lax.Precision.HIGHEST) + b_ref[...]


def _mod_call(c_all, w_ada, b_ada):
    rows, d = c_all.shape
    n = w_ada.shape[1]
    tn = 1024
    return pl.pallas_call(
        _mod_kernel,
        grid=(n // tn,),
        in_specs=[pl.BlockSpec((rows, d), lambda j: (0, 0)),
                  pl.BlockSpec((d, tn), lambda j: (0, j)),
                  pl.BlockSpec((1, tn), lambda j: (0, j))],
        out_specs=pl.BlockSpec((rows, tn), lambda j: (0, j)),
        out_shape=jax.ShapeDtypeStruct((rows, n), F32),
        name="adaln_mod",
    )(c_all, w_ada, b_ada.reshape(1, n))


def _rope_kernel(inv_ref, m1_ref, m2_ref, cos_ref, sa_ref, sb_ref):
    ts = cos_ref.shape[0]
    pos = (pl.program_id(0) * ts + lax.broadcasted_iota(jnp.int32, (ts, HEAD_W), 0)).astype(F32)
    ang = pos * inv_ref[...]
    sin = jnp.sin(ang)
    cos_ref[...] = jnp.cos(ang)
    sa_ref[...] = -sin * m1_ref[...]
    sb_ref[...] = sin * m2_ref[...]


def _rope_call(seq):
    lane = jnp.arange(HEAD_W)
    sub = lane % HEAD_DIM
    inv8 = ROPE_THETA ** (-jnp.arange(ROT_HALF, dtype=F32) / ROT_HALF)
    inv = jnp.where(sub < ROT_DIM, inv8[sub % ROT_HALF], 0.0).astype(F32).reshape(1, HEAD_W)
    m1 = (sub < ROT_HALF).astype(F32).reshape(1, HEAD_W)
    m2 = ((sub >= ROT_HALF) & (sub < ROT_DIM)).astype(F32).reshape(1, HEAD_W)
    ts = min(seq, 2048)
    vec = pl.BlockSpec((1, HEAD_W), lambda i: (0, 0))
    tab = pl.BlockSpec((ts, HEAD_W), lambda i: (i, 0))
    return pl.pallas_call(
        _rope_kernel,
        grid=(seq // ts,),
        in_specs=[vec, vec, vec],
        out_specs=[tab, tab, tab],
        out_shape=[jax.ShapeDtypeStruct((seq, HEAD_W), F32)] * 3,
        name="rope_tables",
    )(inv, m1, m2)


def _group_mean_sq(x):
    sq = x * x
    first = lax.broadcasted_iota(jnp.int32, x.shape, 1) < HEAD_DIM
    s_first = jnp.sum(jnp.where(first, sq, 0.0), axis=1, keepdims=True)
    s_second = jnp.sum(jnp.where(first, 0.0, sq), axis=1, keepdims=True)
    return jnp.where(first, s_first, s_second) * (1.0 / HEAD_DIM)


def _qk_norm_rope(x, gain, cos, sa, sb):
    xn = x * lax.rsqrt(_group_mean_sq(x) + EPS) * gain
    return xn * cos + pltpu.roll(xn, HEAD_W - ROT_HALF, 1) * sa + pltpu.roll(xn, ROT_HALF, 1) * sb


def _inproj_kernel(x_ref, xp_ref, xn_ref, mod_ref, ng_ref, win_ref, convw_ref, qg_ref, kg_ref,
                   cos_ref, sa_ref, sb_ref, wco_ref,
                   q_ref, k_ref, v_ref, sz_ref, a_ref, sgb_ref, *, q_scale):
    i = pl.program_id(1)
    last = pl.num_programs(1) - 1
    tm = x_ref.shape[0]
    shift = mod_ref[0:1, :]
    scale1 = 1.0 + mod_ref[1:2, :]
    ng = ng_ref[...]

    def norm_mod(x):
        ms = jnp.mean(x * x, axis=-1, keepdims=True)
        return (x * lax.rsqrt(ms + EPS) * ng) * scale1 + shift

    h = norm_mod(x_ref[...]).astype(BF16)

    def proj(lo, width):
        return jnp.dot(h, win_ref[:, lo:lo + width], preferred_element_type=F32)

    pqk = proj(_C_Q, 2 * D_ATTN)
    cos, sa, sb = cos_ref[...], sa_ref[...], sb_ref[...]
    qg, kg = qg_ref[...], kg_ref[...]
    for hd in range(N_HEADS):
        lo = hd * HEAD_W
        qh = _qk_norm_rope(pqk[:, lo:lo + HEAD_W], qg, cos, sa, sb)
        q_ref[:, lo:lo + HEAD_W] = (qh * q_scale).astype(BF16)
        kh = _qk_norm_rope(pqk[:, D_ATTN + lo:D_ATTN + lo + HEAD_W], kg, cos, sa, sb)
        k_ref[:, lo:lo + HEAD_W] = kh.astype(BF16)

    hh = norm_mod(jnp.concatenate([xp_ref[...], xn_ref[...]], axis=0)).astype(BF16)
    pc = jnp.dot(jnp.concatenate([h, hh], axis=0), win_ref[:, _C_CONV:_C_CONV + 4 * D_CONV],
                 preferred_element_type=F32)
    cb, cc, cx, cz = (pc[:tm, j * D_CONV:(j + 1) * D_CONV] for j in range(4))
    u = cc * cx
    uh = pc[tm:, D_CONV:2 * D_CONV] * pc[tm:, 2 * D_CONV:3 * D_CONV]
    u_prev = jnp.where(i > 0, uh[HALO_ROWS - 1:HALO_ROWS, :], 0.0)
    u_next = jnp.where(i < last, uh[HALO_ROWS:HALO_ROWS + 1, :], 0.0)
    row = lax.broadcasted_iota(jnp.int32, (tm, D_CONV), 0)
    u_dn = jnp.where(row == 0, u_prev, pltpu.roll(u, 1, 0))
    u_up = jnp.where(row == tm - 1, u_next, pltpu.roll(u, tm - 1, 0))
    cw = convw_ref[...]
    y = cb * (cw[0:1, :] * u_dn + cw[1:2, :] * u + cw[2:3, :] * u_up) * _silu(cz)
    y = y.astype(BF16)

    pv = proj(_C_V, 2 * D_ATTN)
    sz_ref[...] = _silu(pv[:, D_ATTN:]).astype(BF16)
    ones = jnp.ones((tm, HEAD_W), BF16)
    for hd in range(N_HEADS):
        v_ref[:, 2 * hd * HEAD_W:(2 * hd + 1) * HEAD_W] = pv[:, hd * HEAD_W:(hd + 1) * HEAD_W].astype(BF16)
        v_ref[:, (2 * hd + 1) * HEAD_W:(2 * hd + 2) * HEAD_W] = ones

    sga = jax.nn.sigmoid(proj(_C_GA, D_MODEL))
    branch_a = jnp.dot(y, wco_ref[...], preferred_element_type=F32)
    a_ref[...] = (sga * branch_a).astype(BF16)
    sgb_ref[...] = jax.nn.sigmoid(proj(_C_GB, D_MODEL)).astype(BF16)


def _inproj_call(x, mod, norm_g, w_in, conv_w, q_norm_g, k_norm_g, tables, w_conv_out, *, tm):
    b, s, d = x.shape
    nt = s // tm
    hb = tm // HALO_ROWS
    tile = lambda w: pl.BlockSpec((None, tm, w), lambda bi, i: (bi, i, 0))
    const = lambda shape: pl.BlockSpec(shape, lambda bi, i: (0,) * len(shape), pipeline_mode=pl.Buffered(1))
    tab = pl.BlockSpec((tm, HEAD_W), lambda bi, i: (i, 0))
    in_specs = [
        tile(d),
        pl.BlockSpec((None, HALO_ROWS, d), lambda bi, i: (bi, jnp.maximum(i * hb - 1, 0), 0)),
        pl.BlockSpec((None, HALO_ROWS, d), lambda bi, i: (bi, jnp.minimum((i + 1) * hb, nt * hb - 1), 0)),
        pl.BlockSpec((None, 3, d), lambda bi, i: (bi, 0, 0)),
        const((1, d)),
        const(w_in.shape),
        const(conv_w.shape),
        const((1, HEAD_W)),
        const((1, HEAD_W)),
        tab, tab, tab,
        const(w_conv_out.shape),
    ]
    out_specs = [tile(D_ATTN), tile(D_ATTN), tile(2 * D_ATTN), tile(D_ATTN), tile(d), tile(d)]
    out_shape = [jax.ShapeDtypeStruct((b, s, w), BF16)
                 for w in (D_ATTN, D_ATTN, 2 * D_ATTN, D_ATTN, d, d)]
    return pl.pallas_call(
        functools.partial(_inproj_kernel, q_scale=Q_SCALE),
        grid=(b, nt),
        in_specs=in_specs,
        out_specs=out_specs,
        out_shape=out_shape,
        compiler_params=pltpu.CompilerParams(
            dimension_semantics=("parallel", "parallel"),
            vmem_limit_bytes=V7X_VMEM_LIMIT_BYTES),
        name="inproj",
    )(x, x, x, mod, norm_g.reshape(1, d), w_in, conv_w,
      jnp.tile(q_norm_g, 2).reshape(1, HEAD_W), jnp.tile(k_norm_g, 2).reshape(1, HEAD_W),
      *tables, w_conv_out)


def _attn_kernel(bound_ref, q_ref, k_ref, v_ref, sz_ref, lq1_ref, lk1_ref, lq2_ref, lk2_ref, sg_ref,
                 o_ref, acc_ref, m_ref, qz_ref, *, tk, tk_fast, lam_init):
    tq = q_ref.shape[0]
    seq = k_ref.shape[0]
    q = q_ref[...]
    first = lax.broadcasted_iota(jnp.int32, (tq, HEAD_W), 1) < HEAD_DIM
    zero = jnp.zeros_like(q)
    qz_ref[0:tq, :] = jnp.where(first, q, zero)
    qz_ref[tq:2 * tq, :] = jnp.where(first, zero, q)

    bound = bound_ref[0]
    fast = bound <= MAX_FIXED_SHIFT

    def scores(j, width):
        return lax.dot_general(qz_ref[...], k_ref[j * width:(j + 1) * width, :], (((1,), (1,)), ((), ())),
                               preferred_element_type=F32)

    @pl.when(fast)
    def _():
        n = seq // tk_fast
        acc = None
        s_next = scores(0, tk_fast)
        for j in range(n):
            s_cur = s_next
            if j + 1 < n:
                s_next = scores(j + 1, tk_fast)
            p = jnp.exp2(s_cur - bound).astype(BF16)
            d = jnp.dot(p, v_ref[j * tk_fast:(j + 1) * tk_fast, :], preferred_element_type=F32)
            acc = d if acc is None else acc + d
        acc_ref[...] = acc

    @pl.when(jnp.logical_not(fast))
    def _():
        n = seq // tk
        m_ref[...] = jnp.full(m_ref.shape, -jnp.inf, F32)
        acc_ref[...] = jnp.zeros(acc_ref.shape, F32)

        def accumulate(j, s):
            m_prev = m_ref[...]
            m_new = jnp.maximum(m_prev, jnp.max(s, axis=1, keepdims=True))
            alpha = jnp.exp2(m_prev - m_new)
            p = jnp.exp2(s - m_new).astype(BF16)
            acc_ref[...] = alpha * acc_ref[...] + jnp.dot(p, v_ref[j * tk:(j + 1) * tk, :],
                                                          preferred_element_type=F32)
            m_ref[...] = m_new

        s_next = scores(0, tk)
        for j in range(n):
            s_cur = s_next
            if j + 1 < n:
                s_next = scores(j + 1, tk)
            accumulate(j, s_cur)

    acc = acc_ref[...]
    o = acc[:, :HEAD_W] / acc[:, HEAD_W:]
    lam = (jnp.exp(jnp.sum(lq1_ref[...] * lk1_ref[...], axis=1, keepdims=True))
           - jnp.exp(jnp.sum(lq2_ref[...] * lk2_ref[...], axis=1, keepdims=True)) + lam_init)
    od = o[:tq] - lam * o[tq:]
    ms = jnp.mean(od * od, axis=-1, keepdims=True)
    on = od * lax.rsqrt(ms + EPS) * sg_ref[...] * (1.0 - lam_init)
    o_ref[...] = (on * sz_ref[...].astype(F32)).astype(BF16)


def _attn_call(score_bound, q, k, v_ext, sz, lam_q1, lam_k1, lam_q2, lam_k2, subln_g, *, tq, tk, tk_fast,
               lam_init):
    b, s, _ = q.shape
    assert s % tq == 0 and s % tk == 0 and s % tk_fast == 0, (s, tq, tk, tk_fast)
    qtile = pl.BlockSpec((None, tq, HEAD_W), lambda bi, h, i: (bi, i, h))
    vec = lambda w: pl.BlockSpec((1, w), lambda bi, h, i: (0, 0))
    return pl.pallas_call(
        functools.partial(_attn_kernel, tk=tk, tk_fast=tk_fast, lam_init=lam_init),
        grid=(b, N_HEADS, s // tq),
        in_specs=[pl.BlockSpec(memory_space=pltpu.SMEM),
                  qtile,
                  pl.BlockSpec((None, s, HEAD_W), lambda bi, h, i: (bi, 0, h), pipeline_mode=pl.Buffered(1)),
                  pl.BlockSpec((None, s, 2 * HEAD_W), lambda bi, h, i: (bi, 0, h),
                               pipeline_mode=pl.Buffered(1)),
                  qtile,
                  vec(HEAD_DIM), vec(HEAD_DIM), vec(HEAD_DIM), vec(HEAD_DIM), vec(HEAD_W)],
        out_specs=qtile,
        out_shape=jax.ShapeDtypeStruct((b, s, D_ATTN), BF16),
        scratch_shapes=[pltpu.VMEM((2 * tq, 2 * HEAD_W), F32), pltpu.VMEM((2 * tq, 1), F32),
                        pltpu.VMEM((2 * tq, HEAD_W), BF16)],
        compiler_params=pltpu.CompilerParams(
            dimension_semantics=("parallel", "parallel", "parallel"),
            vmem_limit_bytes=V7X_VMEM_LIMIT_BYTES),
        name="diff_attn",
    )(score_bound, q, k, v_ext, sz, lam_q1.reshape(1, -1), lam_k1.reshape(1, -1), lam_q2.reshape(1, -1),
      lam_k2.reshape(1, -1), subln_g.reshape(1, -1))


def _out_kernel(x_ref, og_ref, a_ref, sgb_ref, mod_ref, wao_ref, wo_ref, y_ref):
    branch_b = jnp.dot(og_ref[...], wao_ref[...], preferred_element_type=F32)
    merged = a_ref[...].astype(F32) + sgb_ref[...].astype(F32) * branch_b
    z = jnp.dot(merged.astype(BF16), wo_ref[...], preferred_element_type=F32)
    y_ref[...] = x_ref[...] + mod_ref[2:3, :] * z


def _out_call(x, og, a, sgb, mod, w_attn_out, w_out, *, tm):
    b, s, d = x.shape
    tile = lambda w: pl.BlockSpec((None, tm, w), lambda bi, i: (bi, i, 0))
    const = lambda shape: pl.BlockSpec(shape, lambda bi, i: (0,) * len(shape))
    return pl.pallas_call(
        _out_kernel,
        grid=(b, s // tm),
        in_specs=[tile(d), tile(D_ATTN), tile(d), tile(d),
                  pl.BlockSpec((None, 3, d), lambda bi, i: (bi, 0, 0)),
                  const(w_attn_out.shape), const(w_out.shape)],
        out_specs=tile(d),
        out_shape=jax.ShapeDtypeStruct((b, s, d), F32),
        compiler_params=pltpu.CompilerParams(
            dimension_semantics=("parallel", "parallel"),
            vmem_limit_bytes=V7X_VMEM_LIMIT_BYTES),
        name="out_proj",
    )(x, og, a, sgb, mod, w_attn_out, w_out)


def _tiles(seq):
    if seq <= 2048:
        return dict(tm=min(seq, 512), to=min(seq, 1024), tq=min(seq, 512), tk=min(seq, 1024), tk_fast=min(seq, 256))
    return dict(tm=512, to=1024, tq=512, tk=2048, tk_fast=256)


def _layer(x, mod, layer_idx, tables, norm_g, w_in, conv_w, q_norm_g, k_norm_g, lam_q1, lam_k1, lam_q2,
           lam_k2, subln_g, w_conv_out, w_attn_out, w_out):
    t = _tiles(x.shape[1])
    lam_init = 0.8 - 0.6 * math.exp(-0.3 * layer_idx)
    q, k, v_ext, sz, a, sgb = _inproj_call(x, mod, norm_g, w_in, conv_w, q_norm_g, k_norm_g, tables,
                                           w_conv_out, tm=t["tm"])
    score_bound = (HEAD_DIM * Q_SCALE * jnp.max(jnp.abs(q_norm_g)) * jnp.max(jnp.abs(k_norm_g))).reshape(1)
    og = _attn_call(score_bound.astype(F32), q, k, v_ext, sz, lam_q1, lam_k1, lam_q2, lam_k2, subln_g,
                    tq=t["tq"], tk=t["tk"], tk_fast=t["tk_fast"], lam_init=lam_init)
    return _out_call(x, og, a, sgb, mod, w_attn_out, w_out, tm=t["to"])


def kernel(x_prompt, x_sample, c_prompt, c_sample, norm_g, w_ada, b_ada, w_in, conv_w, q_norm_g, k_norm_g,
           lam_q1, lam_k1, lam_q2, lam_k2, subln_g, w_conv_out, w_attn_out, w_out):
    depth = norm_g.shape[0]
    groups = [(x_prompt, c_prompt), (x_sample, c_sample)]
    n_rows = sum(c.shape[0] for _, c in groups)
    pad = -n_rows % 8
    c_all = jnp.concatenate([c for _, c in groups] + [jnp.zeros((pad, D_MODEL), F32)], axis=0)
    tables = _rope_call(max(x.shape[1] for x, _ in groups))
    xs = [x for x, _ in groups]
    for l in range(depth):
        mod_all = _mod_call(c_all, w_ada[l], b_ada[l])
        w_in_b, wco_b, wao_b, wo_b = (w[l].astype(BF16) for w in (w_in, w_conv_out, w_attn_out, w_out))
        row = 0
        for gi, (_, c) in enumerate(groups):
            nb = c.shape[0]
            mod = mod_all[row:row + nb].reshape(nb, 3, D_MODEL)
            row += nb
            xs[gi] = _layer(xs[gi], mod, l, tables, norm_g[l], w_in_b, conv_w[l],
                            q_norm_g[l], k_norm_g[l], lam_q1[l], lam_k1[l], lam_q2[l], lam_k2[l],
                            subln_g[l], wco_b, wao_b, wo_b)
    return tuple(xs)
```

```python
import functools
import math

import jax
import jax.numpy as jnp
from jax import lax
from jax.experimental import pallas as pl
from jax.experimental.pallas import tpu as pltpu

F32 = jnp.float32
BF16 = jnp.bfloat16

D_MODEL = 1024
D_CONV = 512
N_HEADS = 4
HEAD_DIM = 64
HEAD_W = 2 * HEAD_DIM
D_ATTN = N_HEADS * HEAD_W
ROT_DIM = HEAD_DIM // 4
ROT_HALF = ROT_DIM // 2
ROPE_THETA = 500000.0
EPS = 1e-6
LOG2E = 1.4426950408889634
Q_SCALE = HEAD_DIM ** -0.5 * LOG2E

_C_CONV = 0
_C_Q = 4 * D_CONV
_C_K = _C_Q + D_ATTN
_C_V = _C_K + D_ATTN
_C_AZ = _C_V + D_ATTN
_C_GA = _C_AZ + D_ATTN
_C_GB = _C_GA + D_MODEL

V7X_VMEM_LIMIT_BYTES = 56 * 1024 * 1024
HALO_ROWS = 8
ROPE_BLOCK = 128
MAX_FIXED_SHIFT = 48.0


def _silu(z):
    return z * jax.nn.sigmoid(z)


def _mod_kernel(c_ref, w_ref, b_ref, o_ref):
    s = _silu(c_ref[...])
    o_ref[...] = jnp.dot(s, w_ref[...], preferred_element_type=F32,
                         precision=lax.Precision.HIGHEST) + b_ref[...]


def _mod_call(c_all, w_ada, b_ada):
    rows, d = c_all.shape
    n = w_ada.shape[1]
    tn = 1024
    return pl.pallas_call(
        _mod_kernel,
        grid=(n // tn,),
        in_specs=[pl.BlockSpec((rows, d), lambda j: (0, 0)),
                  pl.BlockSpec((d, tn), lambda j: (0, j)),
                  pl.BlockSpec((1, tn), lambda j: (0, j))],
        out_specs=pl.BlockSpec((rows, tn), lambda j: (0, j)),
        out_shape=jax.ShapeDtypeStruct((rows, n), F32),
        name="adaln_mod",
    )(c_all, w_ada, b_ada.reshape(1, n))


def _rope_kernel(inv_ref, m1_ref, m2_ref, cos_ref, sa_ref, sb_ref, clo_ref, slo_ref, chi_ref, shi_ref):
    i = pl.program_id(0)
    blocks = cos_ref.shape[0] // ROPE_BLOCK

    @pl.when(i == 0)
    def _():
        inv = inv_ref[...]
        lo = lax.broadcasted_iota(jnp.int32, clo_ref.shape, 0).astype(F32) * inv
        clo_ref[...] = jnp.cos(lo)
        slo_ref[...] = jnp.sin(lo)
        hi = (lax.broadcasted_iota(jnp.int32, chi_ref.shape, 0) * ROPE_BLOCK).astype(F32) * inv
        chi_ref[...] = jnp.cos(hi)
        shi_ref[...] = jnp.sin(hi)

    clo, slo = clo_ref[...], slo_ref[...]
    m1, m2 = m1_ref[...], m2_ref[...]
    for a in range(blocks):
        blk = i * blocks + a
        chi = chi_ref[pl.ds(blk, 1), :]
        shi = shi_ref[pl.ds(blk, 1), :]
        sin = shi * clo + chi * slo
        rows = slice(a * ROPE_BLOCK, (a + 1) * ROPE_BLOCK)
        cos_ref[rows, :] = chi * clo - shi * slo
        sa_ref[rows, :] = -sin * m1
        sb_ref[rows, :] = sin * m2


def _rope_call(seq):
    lane = jnp.arange(HEAD_W)
    sub = lane % HEAD_DIM
    inv8 = ROPE_THETA ** (-jnp.arange(ROT_HALF, dtype=F32) / ROT_HALF)
    inv = jnp.where(sub < ROT_DIM, inv8[sub % ROT_HALF], 0.0).astype(F32).reshape(1, HEAD_W)
    m1 = (sub < ROT_HALF).astype(F32).reshape(1, HEAD_W)
    m2 = ((sub >= ROT_HALF) & (sub < ROT_DIM)).astype(F32).reshape(1, HEAD_W)
    ts = min(seq, 1024)
    assert seq % ts == 0 and ts % ROPE_BLOCK == 0, (seq, ts)
    vec = pl.BlockSpec((1, HEAD_W), lambda i: (0, 0))
    tab = pl.BlockSpec((ts, HEAD_W), lambda i: (i, 0))
    return pl.pallas_call(
        _rope_kernel,
        grid=(seq // ts,),
        in_specs=[vec, vec, vec],
        out_specs=[tab, tab, tab],
        out_shape=[jax.ShapeDtypeStruct((seq, HEAD_W), F32)] * 3,
        scratch_shapes=[pltpu.VMEM((ROPE_BLOCK, HEAD_W), F32), pltpu.VMEM((ROPE_BLOCK, HEAD_W), F32),
                        pltpu.VMEM((seq // ROPE_BLOCK, HEAD_W), F32), pltpu.VMEM((seq // ROPE_BLOCK, HEAD_W), F32)],
        compiler_params=pltpu.CompilerParams(dimension_semantics=("arbitrary",)),
        name="rope_tables",
    )(inv, m1, m2)


def _group_mean_sq(x):
    sq = x * x
    first = lax.broadcasted_iota(jnp.int32, x.shape, 1) < HEAD_DIM
    s_first = jnp.sum(jnp.where(first, sq, 0.0), axis=1, keepdims=True)
    s_second = jnp.sum(jnp.where(first, 0.0, sq), axis=1, keepdims=True)
    return jnp.where(first, s_first, s_second) * (1.0 / HEAD_DIM)


def _qk_norm_rope(x, gain, cos, sa, sb):
    xn = x * lax.rsqrt(_group_mean_sq(x) + EPS) * gain
    return xn * cos + pltpu.roll(xn, HEAD_W - ROT_HALF, 1) * sa + pltpu.roll(xn, ROT_HALF, 1) * sb


def _inproj_kernel(x_ref, xp_ref, xn_ref, mod_ref, ng_ref, win_ref, convw_ref, qg_ref, kg_ref,
                   cos_ref, sa_ref, sb_ref, wco_ref,
                   q_ref, k_ref, v_ref, sz_ref, a_ref, sgb_ref, *, q_scale):
    i = pl.program_id(1)
    last = pl.num_programs(1) - 1
    tm = x_ref.shape[0]
    shift = mod_ref[0:1, :]
    scale1 = 1.0 + mod_ref[1:2, :]
    ng = ng_ref[...]

    def norm_mod(x):
        ms = jnp.mean(x * x, axis=-1, keepdims=True)
        return (x * lax.rsqrt(ms + EPS) * ng) * scale1 + shift

    h = norm_mod(x_ref[...]).astype(BF16)

    def proj(lo, width):
        return jnp.dot(h, win_ref[:, lo:lo + width], preferred_element_type=F32)

    pqk = proj(_C_Q, 2 * D_ATTN)
    cos, sa, sb = cos_ref[...], sa_ref[...], sb_ref[...]
    qg, kg = qg_ref[...], kg_ref[...]
    for hd in range(N_HEADS):
        lo = hd * HEAD_W
        qh = _qk_norm_rope(pqk[:, lo:lo + HEAD_W], qg, cos, sa, sb)
        q_ref[:, lo:lo + HEAD_W] = (qh * q_scale).astype(BF16)
        kh = _qk_norm_rope(pqk[:, D_ATTN + lo:D_ATTN + lo + HEAD_W], kg, cos, sa, sb)
        k_ref[:, lo:lo + HEAD_W] = kh.astype(BF16)

    hh = norm_mod(jnp.concatenate([xp_ref[...], xn_ref[...]], axis=0)).astype(BF16)
    pc = jnp.dot(jnp.concatenate([h, hh], axis=0), win_ref[:, _C_CONV:_C_CONV + 4 * D_CONV],
                 preferred_element_type=F32)
    cb, cc, cx, cz = (pc[:tm, j * D_CONV:(j + 1) * D_CONV] for j in range(4))
    u = cc * cx
    uh = pc[tm:, D_CONV:2 * D_CONV] * pc[tm:, 2 * D_CONV:3 * D_CONV]
    u_prev = jnp.where(i > 0, uh[HALO_ROWS - 1:HALO_ROWS, :], 0.0)
    u_next = jnp.where(i < last, uh[HALO_ROWS:HALO_ROWS + 1, :], 0.0)
    row = lax.broadcasted_iota(jnp.int32, (tm, D_CONV), 0)
    u_dn = jnp.where(row == 0, u_prev, pltpu.roll(u, 1, 0))
    u_up = jnp.where(row == tm - 1, u_next, pltpu.roll(u, tm - 1, 0))
    cw = convw_ref[...]
    y = cb * (cw[0:1, :] * u_dn + cw[1:2, :] * u + cw[2:3, :] * u_up) * _silu(cz)
    y = y.astype(BF16)

    pv = proj(_C_V, 2 * D_ATTN)
    sz_ref[...] = _silu(pv[:, D_ATTN:]).astype(BF16)
    ones = jnp.ones((tm, HEAD_W), BF16)
    for hd in range(N_HEADS):
        v_ref[:, 2 * hd * HEAD_W:(2 * hd + 1) * HEAD_W] = pv[:, hd * HEAD_W:(hd + 1) * HEAD_W].astype(BF16)
        v_ref[:, (2 * hd + 1) * HEAD_W:(2 * hd + 2) * HEAD_W] = ones

    sga = jax.nn.sigmoid(proj(_C_GA, D_MODEL))
    branch_a = jnp.dot(y, wco_ref[...], preferred_element_type=F32)
    a_ref[...] = (sga * branch_a).astype(BF16)
    sgb_ref[...] = jax.nn.sigmoid(proj(_C_GB, D_MODEL)).astype(BF16)


def _inproj_call(x, mod, norm_g, w_in, conv_w, q_norm_g, k_norm_g, tables, w_conv_out, *, tm):
    b, s, d = x.shape
    nt = s // tm
    hb = tm // HALO_ROWS
    tile = lambda w: pl.BlockSpec((None, tm, w), lambda bi, i: (bi, i, 0))
    const = lambda shape: pl.BlockSpec(shape, lambda bi, i: (0,) * len(shape), pipeline_mode=pl.Buffered(1))
    tab = pl.BlockSpec((tm, HEAD_W), lambda bi, i: (i, 0))
    in_specs = [
        tile(d),
        pl.BlockSpec((None, HALO_ROWS, d), lambda bi, i: (bi, jnp.maximum(i * hb - 1, 0), 0)),
        pl.BlockSpec((None, HALO_ROWS, d), lambda bi, i: (bi, jnp.minimum((i + 1) * hb, nt * hb - 1), 0)),
        pl.BlockSpec((None, 3, d), lambda bi, i: (bi, 0, 0)),
        const((1, d)),
        const(w_in.shape),
        const(conv_w.shape),
        const((1, HEAD_W)),
        const((1, HEAD_W)),
        tab, tab, tab,
        const(w_conv_out.shape),
    ]
    out_specs = [tile(D_ATTN), tile(D_ATTN), tile(2 * D_ATTN), tile(D_ATTN), tile(d), tile(d)]
    out_shape = [jax.ShapeDtypeStruct((b, s, w), BF16)
                 for w in (D_ATTN, D_ATTN, 2 * D_ATTN, D_ATTN, d, d)]
    return pl.pallas_call(
        functools.partial(_inproj_kernel, q_scale=Q_SCALE),
        grid=(b, nt),
        in_specs=in_specs,
        out_specs=out_specs,
        out_shape=out_shape,
        compiler_params=pltpu.CompilerParams(
            dimension_semantics=("parallel", "parallel"),
            vmem_limit_bytes=V7X_VMEM_LIMIT_BYTES),
        name="inproj",
    )(x, x, x, mod, norm_g.reshape(1, d), w_in, conv_w,
      jnp.tile(q_norm_g, 2).reshape(1, HEAD_W), jnp.tile(k_norm_g, 2).reshape(1, HEAD_W),
      *tables, w_conv_out)


def _attn_kernel(bound_ref, q_ref, k_ref, v_ref, sz_ref, lq1_ref, lk1_ref, lq2_ref, lk2_ref, sg_ref,
                 o_ref, acc_ref, m_ref, qz_ref, lam_ref, *, tk, tk_fast, lam_init):
    tq = q_ref.shape[0]
    seq = k_ref.shape[0]
    q = q_ref[...]
    first = lax.broadcasted_iota(jnp.int32, (tq, HEAD_W), 1) < HEAD_DIM
    zero = jnp.zeros_like(q)
    qz_ref[0:tq, :] = jnp.where(first, q, zero)
    qz_ref[tq:2 * tq, :] = jnp.where(first, zero, q)

    bound = bound_ref[0]
    fast = bound <= MAX_FIXED_SHIFT

    def store_lambda():
        lam = (jnp.exp(jnp.sum(lq1_ref[...] * lk1_ref[...], axis=1, keepdims=True))
               - jnp.exp(jnp.sum(lq2_ref[...] * lk2_ref[...], axis=1, keepdims=True)) + lam_init)
        lam_ref[...] = jnp.broadcast_to(lam, lam_ref.shape)

    def scores(j, width):
        return lax.dot_general(qz_ref[...], k_ref[j * width:(j + 1) * width, :], (((1,), (1,)), ((), ())),
                               preferred_element_type=F32)

    @pl.when(fast)
    def _():
        store_lambda()
        n = seq // tk_fast
        acc = None
        s_next = scores(0, tk_fast)
        for j in range(n):
            s_cur = s_next
            if j + 1 < n:
                s_next = scores(j + 1, tk_fast)
            p = jnp.exp2(s_cur - bound).astype(BF16)
            d = jnp.dot(p, v_ref[j * tk_fast:(j + 1) * tk_fast, :], preferred_element_type=F32)
            acc = d if acc is None else acc + d
        acc_ref[...] = acc

    @pl.when(jnp.logical_not(fast))
    def _():
        store_lambda()
        n = seq // tk
        m_ref[...] = jnp.full(m_ref.shape, -jnp.inf, F32)
        acc_ref[...] = jnp.zeros(acc_ref.shape, F32)

        def accumulate(j, s):
            m_prev = m_ref[...]
            m_new = jnp.maximum(m_prev, jnp.max(s, axis=1, keepdims=True))
            alpha = jnp.exp2(m_prev - m_new)
            p = jnp.exp2(s - m_new).astype(BF16)
            acc_ref[...] = alpha * acc_ref[...] + jnp.dot(p, v_ref[j * tk:(j + 1) * tk, :],
                                                          preferred_element_type=F32)
            m_ref[...] = m_new

        s_next = scores(0, tk)
        for j in range(n):
            s_cur = s_next
            if j + 1 < n:
                s_next = scores(j + 1, tk)
            accumulate(j, s_cur)

    acc = acc_ref[...]
    o = acc[:, :HEAD_W] / acc[:, HEAD_W:]
    od = o[:tq] - lam_ref[...] * o[tq:]
    ms = jnp.mean(od * od, axis=-1, keepdims=True)
    on = od * lax.rsqrt(ms + EPS) * sg_ref[...] * (1.0 - lam_init)
    o_ref[...] = (on * sz_ref[...].astype(F32)).astype(BF16)


def _attn_call(score_bound, q, k, v_ext, sz, lam_q1, lam_k1, lam_q2, lam_k2, subln_g, *, tq, tk, tk_fast,
               lam_init):
    b, s, _ = q.shape
    assert s % tq == 0 and s % tk == 0 and s % tk_fast == 0, (s, tq, tk, tk_fast)
    qtile = pl.BlockSpec((None, tq, HEAD_W), lambda bi, h, i: (bi, i, h))
    vec = lambda w: pl.BlockSpec((1, w), lambda bi, h, i: (0, 0))
    return pl.pallas_call(
        functools.partial(_attn_kernel, tk=tk, tk_fast=tk_fast, lam_init=lam_init),
        grid=(b, N_HEADS, s // tq),
        in_specs=[pl.BlockSpec(memory_space=pltpu.SMEM),
                  qtile,
                  pl.BlockSpec((None, s, HEAD_W), lambda bi, h, i: (bi, 0, h)),
                  pl.BlockSpec((None, s, 2 * HEAD_W), lambda bi, h, i: (bi, 0, h)),
                  qtile,
                  vec(HEAD_DIM), vec(HEAD_DIM), vec(HEAD_DIM), vec(HEAD_DIM), vec(HEAD_W)],
        out_specs=qtile,
        out_shape=jax.ShapeDtypeStruct((b, s, D_ATTN), BF16),
        scratch_shapes=[pltpu.VMEM((2 * tq, 2 * HEAD_W), F32), pltpu.VMEM((2 * tq, 1), F32),
                        pltpu.VMEM((2 * tq, HEAD_W), BF16), pltpu.VMEM((1, HEAD_W), F32)],
        compiler_params=pltpu.CompilerParams(
            dimension_semantics=("parallel", "parallel", "parallel"),
            vmem_limit_bytes=V7X_VMEM_LIMIT_BYTES),
        name="diff_attn",
    )(score_bound, q, k, v_ext, sz, lam_q1.reshape(1, -1), lam_k1.reshape(1, -1), lam_q2.reshape(1, -1),
      lam_k2.reshape(1, -1), subln_g.reshape(1, -1))


def _out_kernel(x_ref, og_ref, a_ref, sgb_ref, mod_ref, wao_ref, wo_ref, y_ref):
    branch_b = jnp.dot(og_ref[...], wao_ref[...], preferred_element_type=F32)
    merged = a_ref[...].astype(F32) + sgb_ref[...].astype(F32) * branch_b
    z = jnp.dot(merged.astype(BF16), wo_ref[...], preferred_element_type=F32)
    y_ref[...] = x_ref[...] + mod_ref[2:3, :] * z


def _out_call(x, og, a, sgb, mod, w_attn_out, w_out, *, tm):
    b, s, d = x.shape
    tile = lambda w: pl.BlockSpec((None, tm, w), lambda bi, i: (bi, i, 0))
    const = lambda shape: pl.BlockSpec(shape, lambda bi, i: (0,) * len(shape))
    return pl.pallas_call(
        _out_kernel,
        grid=(b, s // tm),
        in_specs=[tile(d), tile(D_ATTN), tile(d), tile(d),
                  pl.BlockSpec((None, 3, d), lambda bi, i: (bi, 0, 0)),
                  const(w_attn_out.shape), const(w_out.shape)],
        out_specs=tile(d),
        out_shape=jax.ShapeDtypeStruct((b, s, d), F32),
        compiler_params=pltpu.CompilerParams(
            dimension_semantics=("parallel", "parallel"),
            vmem_limit_bytes=V7X_VMEM_LIMIT_BYTES),
        name="out_proj",
    )(x, og, a, sgb, mod, w_attn_out, w_out)


def _tiles(seq):
    if seq <= 2048:
        return dict(tm=min(seq, 512), to=min(seq, 1024), tq=min(seq, 1024), tk=min(seq, 1024), tk_fast=min(seq, 256))
    return dict(tm=512, to=1024, tq=256, tk=4096, tk_fast=256)


def _layer(x, mod, layer_idx, tables, norm_g, w_in, conv_w, q_norm_g, k_norm_g, lam_q1, lam_k1, lam_q2,
           lam_k2, subln_g, w_conv_out, w_attn_out, w_out):
    t = _tiles(x.shape[1])
    lam_init = 0.8 - 0.6 * math.exp(-0.3 * layer_idx)
    q, k, v_ext, sz, a, sgb = _inproj_call(x, mod, norm_g, w_in, conv_w, q_norm_g, k_norm_g, tables,
                                           w_conv_out, tm=t["tm"])
    score_bound = (HEAD_DIM * Q_SCALE * jnp.max(jnp.abs(q_norm_g)) * jnp.max(jnp.abs(k_norm_g))).reshape(1)
    og = _attn_call(score_bound.astype(F32), q, k, v_ext, sz, lam_q1, lam_k1, lam_q2, lam_k2, subln_g,
                    tq=t["tq"], tk=t["tk"], tk_fast=t["tk_fast"], lam_init=lam_init)
    return _out_call(x, og, a, sgb, mod, w_attn_out, w_out, tm=t["to"])


def kernel(x_prompt, x_sample, c_prompt, c_sample, norm_g, w_ada, b_ada, w_in, conv_w, q_norm_g, k_norm_g,
           lam_q1, lam_k1, lam_q2, lam_k2, subln_g, w_conv_out, w_attn_out, w_out):
    depth = norm_g.shape[0]
    groups = [(x_prompt, c_prompt), (x_sample, c_sample)]
    n_rows = sum(c.shape[0] for _, c in groups)
    pad = -n_rows % 8
    c_all = jnp.concatenate([c for _, c in groups] + [jnp.zeros((pad, D_MODEL), F32)], axis=0)
    tables = _rope_call(max(x.shape[1] for x, _ in groups))
    xs = [x for x, _ in groups]
    for l in range(depth):
        mod_all = _mod_call(c_all, w_ada[l], b_ada[l])
        w_in_b, wco_b, wao_b, wo_b = (w[l].astype(BF16) for w in (w_in, w_conv_out, w_attn_out, w_out))
        row = 0
        for gi, (_, c) in enumerate(groups):
            nb = c.shape[0]
            mod = mod_all[row:row + nb].reshape(nb, 3, D_MODEL)
            row += nb
            xs[gi] = _layer(xs[gi], mod, l, tables, norm_g[l], w_in_b, conv_w[l],
                            q_norm_g[l], k_norm_g[l], lam_q1[l], lam_k1[l], lam_q2[l], lam_k2[l],
                            subln_g[l], wco_b, wao_b, wo_b)
    return tuple(xs)
```

```python
import functools
import math

import jax
import jax.numpy as jnp
from jax import lax
from jax.experimental import pallas as pl
from jax.experimental.pallas import tpu as pltpu

F32 = jnp.float32
BF16 = jnp.bfloat16

D_MODEL = 1024
D_CONV = 512
N_HEADS = 4
HEAD_DIM = 64
HEAD_W = 2 * HEAD_DIM
D_ATTN = N_HEADS * HEAD_W
ROT_DIM = HEAD_DIM // 4
ROT_HALF = ROT_DIM // 2
ROPE_THETA = 500000.0
EPS = 1e-6
LOG2E = 1.4426950408889634
Q_SCALE = HEAD_DIM ** -0.5 * LOG2E

_C_CONV = 0
_C_Q = 4 * D_CONV
_C_K = _C_Q + D_ATTN
_C_V = _C_K + D_ATTN
_C_AZ = _C_V + D_ATTN
_C_GA = _C_AZ + D_ATTN
_C_GB = _C_GA + D_MODEL

V7X_VMEM_LIMIT_BYTES = 56 * 1024 * 1024
HALO_ROWS = 8
ROPE_BLOCK = 128
MAX_UNSHIFTED_SCORE = 48.0


def _silu(z):
    return z * jax.nn.sigmoid(z)


def _mod_kernel(c_ref, w_ref, b_ref, o_ref):
    s = _silu(c_ref[...])
    o_ref[...] = jnp.dot(s, w_ref[...], preferred_element_type=F32,
                         precision=lax.Precision.HIGHEST) + b_ref[...]


def _mod_call(c_all, w_ada, b_ada):
    rows, d = c_all.shape
    n = w_ada.shape[1]
    tn = 1024
    return pl.pallas_call(
        _mod_kernel,
        grid=(n // tn,),
        in_specs=[pl.BlockSpec((rows, d), lambda j: (0, 0)),
                  pl.BlockSpec((d, tn), lambda j: (0, j)),
                  pl.BlockSpec((1, tn), lambda j: (0, j))],
        out_specs=pl.BlockSpec((rows, tn), lambda j: (0, j)),
        out_shape=jax.ShapeDtypeStruct((rows, n), F32),
        name="adaln_mod",
    )(c_all, w_ada, b_ada.reshape(1, n))


def _rope_kernel(inv_ref, m1_ref, m2_ref, cos_ref, sa_ref, sb_ref, clo_ref, slo_ref, chi_ref, shi_ref):
    i = pl.program_id(0)
    blocks = cos_ref.shape[0] // ROPE_BLOCK

    @pl.when(i == 0)
    def _():
        inv = inv_ref[...]
        lo = lax.broadcasted_iota(jnp.int32, clo_ref.shape, 0).astype(F32) * inv
        clo_ref[...] = jnp.cos(lo)
        slo_ref[...] = jnp.sin(lo)
        hi = (lax.broadcasted_iota(jnp.int32, chi_ref.shape, 0) * ROPE_BLOCK).astype(F32) * inv
        chi_ref[...] = jnp.cos(hi)
        shi_ref[...] = jnp.sin(hi)

    clo, slo = clo_ref[...], slo_ref[...]
    m1, m2 = m1_ref[...], m2_ref[...]
    for a in range(blocks):
        blk = i * blocks + a
        chi = chi_ref[pl.ds(blk, 1), :]
        shi = shi_ref[pl.ds(blk, 1), :]
        sin = shi * clo + chi * slo
        rows = slice(a * ROPE_BLOCK, (a + 1) * ROPE_BLOCK)
        cos_ref[rows, :] = chi * clo - shi * slo
        sa_ref[rows, :] = -sin * m1
        sb_ref[rows, :] = sin * m2


def _rope_call(seq):
    lane = jnp.arange(HEAD_W)
    sub = lane % HEAD_DIM
    inv8 = ROPE_THETA ** (-jnp.arange(ROT_HALF, dtype=F32) / ROT_HALF)
    inv = jnp.where(sub < ROT_DIM, inv8[sub % ROT_HALF], 0.0).astype(F32).reshape(1, HEAD_W)
    m1 = (sub < ROT_HALF).astype(F32).reshape(1, HEAD_W)
    m2 = ((sub >= ROT_HALF) & (sub < ROT_DIM)).astype(F32).reshape(1, HEAD_W)
    ts = min(seq, 1024)
    assert seq % ts == 0 and ts % ROPE_BLOCK == 0, (seq, ts)
    vec = pl.BlockSpec((1, HEAD_W), lambda i: (0, 0))
    tab = pl.BlockSpec((ts, HEAD_W), lambda i: (i, 0))
    return pl.pallas_call(
        _rope_kernel,
        grid=(seq // ts,),
        in_specs=[vec, vec, vec],
        out_specs=[tab, tab, tab],
        out_shape=[jax.ShapeDtypeStruct((seq, HEAD_W), F32)] * 3,
        scratch_shapes=[pltpu.VMEM((ROPE_BLOCK, HEAD_W), F32), pltpu.VMEM((ROPE_BLOCK, HEAD_W), F32),
                        pltpu.VMEM((seq // ROPE_BLOCK, HEAD_W), F32), pltpu.VMEM((seq // ROPE_BLOCK, HEAD_W), F32)],
        compiler_params=pltpu.CompilerParams(dimension_semantics=("arbitrary",)),
        name="rope_tables",
    )(inv, m1, m2)


def _group_mean_sq(x):
    sq = x * x
    first = lax.broadcasted_iota(jnp.int32, x.shape, 1) < HEAD_DIM
    s_first = jnp.sum(jnp.where(first, sq, 0.0), axis=1, keepdims=True)
    s_second = jnp.sum(jnp.where(first, 0.0, sq), axis=1, keepdims=True)
    return jnp.where(first, s_first, s_second) * (1.0 / HEAD_DIM)


def _qk_norm_rope(x, gain, cos, sa, sb):
    xn = x * lax.rsqrt(_group_mean_sq(x) + EPS) * gain
    return xn * cos + pltpu.roll(xn, HEAD_W - ROT_HALF, 1) * sa + pltpu.roll(xn, ROT_HALF, 1) * sb


def _inproj_kernel(x_ref, xp_ref, xn_ref, mod_ref, ng_ref, win_ref, convw_ref, qg_ref, kg_ref,
                   cos_ref, sa_ref, sb_ref, wco_ref,
                   q_ref, k_ref, v_ref, sz_ref, a_ref, sgb_ref, *, q_scale):
    i = pl.program_id(1)
    last = pl.num_programs(1) - 1
    tm = x_ref.shape[0]
    shift = mod_ref[0:1, :]
    scale1 = 1.0 + mod_ref[1:2, :]
    ng = ng_ref[...]

    def norm_mod(x):
        ms = jnp.mean(x * x, axis=-1, keepdims=True)
        return (x * lax.rsqrt(ms + EPS) * ng) * scale1 + shift

    h = norm_mod(x_ref[...]).astype(BF16)

    def proj(lo, width):
        return jnp.dot(h, win_ref[:, lo:lo + width], preferred_element_type=F32)

    pqk = proj(_C_Q, 2 * D_ATTN)
    cos, sa, sb = cos_ref[...], sa_ref[...], sb_ref[...]
    qg, kg = qg_ref[...], kg_ref[...]
    for hd in range(N_HEADS):
        lo = hd * HEAD_W
        qh = _qk_norm_rope(pqk[:, lo:lo + HEAD_W], qg, cos, sa, sb)
        q_ref[:, lo:lo + HEAD_W] = (qh * q_scale).astype(BF16)
        kh = _qk_norm_rope(pqk[:, D_ATTN + lo:D_ATTN + lo + HEAD_W], kg, cos, sa, sb)
        k_ref[:, lo:lo + HEAD_W] = kh.astype(BF16)

    hh = norm_mod(jnp.concatenate([xp_ref[...], xn_ref[...]], axis=0)).astype(BF16)
    pc = jnp.dot(jnp.concatenate([h, hh], axis=0), win_ref[:, _C_CONV:_C_CONV + 4 * D_CONV],
                 preferred_element_type=F32)
    cb, cc, cx, cz = (pc[:tm, j * D_CONV:(j + 1) * D_CONV] for j in range(4))
    u = cc * cx
    uh = pc[tm:, D_CONV:2 * D_CONV] * pc[tm:, 2 * D_CONV:3 * D_CONV]
    u_prev = jnp.where(i > 0, uh[HALO_ROWS - 1:HALO_ROWS, :], 0.0)
    u_next = jnp.where(i < last, uh[HALO_ROWS:HALO_ROWS + 1, :], 0.0)
    row = lax.broadcasted_iota(jnp.int32, (tm, D_CONV), 0)
    u_dn = jnp.where(row == 0, u_prev, pltpu.roll(u, 1, 0))
    u_up = jnp.where(row == tm - 1, u_next, pltpu.roll(u, tm - 1, 0))
    cw = convw_ref[...]
    y = cb * (cw[0:1, :] * u_dn + cw[1:2, :] * u + cw[2:3, :] * u_up) * _silu(cz)
    y = y.astype(BF16)

    sz_ref[...] = _silu(proj(_C_AZ, D_ATTN)).astype(BF16)

    sga = jax.nn.sigmoid(proj(_C_GA, D_MODEL))
    branch_a = jnp.dot(y, wco_ref[...], preferred_element_type=F32)
    a_ref[...] = (sga * branch_a).astype(BF16)
    sgb_ref[...] = jax.nn.sigmoid(proj(_C_GB, D_MODEL)).astype(BF16)

    pv = proj(_C_V, D_ATTN)
    ones = jnp.ones((tm, HEAD_W), BF16)
    for hd in range(N_HEADS):
        v_ref[:, 2 * hd * HEAD_W:(2 * hd + 1) * HEAD_W] = pv[:, hd * HEAD_W:(hd + 1) * HEAD_W].astype(BF16)
        v_ref[:, (2 * hd + 1) * HEAD_W:(2 * hd + 2) * HEAD_W] = ones


def _inproj_call(x, mod, norm_g, w_in, conv_w, q_norm_g, k_norm_g, tables, w_conv_out, *, tm):
    b, s, d = x.shape
    nt = s // tm
    hb = tm // HALO_ROWS
    tile = lambda w: pl.BlockSpec((None, tm, w), lambda bi, i: (bi, i, 0))
    const = lambda shape: pl.BlockSpec(shape, lambda bi, i: (0,) * len(shape), pipeline_mode=pl.Buffered(1))
    tab = pl.BlockSpec((tm, HEAD_W), lambda bi, i: (i, 0))
    in_specs = [
        tile(d),
        pl.BlockSpec((None, HALO_ROWS, d), lambda bi, i: (bi, jnp.maximum(i * hb - 1, 0), 0)),
        pl.BlockSpec((None, HALO_ROWS, d), lambda bi, i: (bi, jnp.minimum((i + 1) * hb, nt * hb - 1), 0)),
        pl.BlockSpec((None, 3, d), lambda bi, i: (bi, 0, 0)),
        const((1, d)),
        const(w_in.shape),
        const(conv_w.shape),
        const((1, HEAD_W)),
        const((1, HEAD_W)),
        tab, tab, tab,
        const(w_conv_out.shape),
    ]
    out_specs = [tile(D_ATTN), tile(D_ATTN), tile(2 * D_ATTN), tile(D_ATTN), tile(d), tile(d)]
    out_shape = [jax.ShapeDtypeStruct((b, s, w), BF16)
                 for w in (D_ATTN, D_ATTN, 2 * D_ATTN, D_ATTN, d, d)]
    return pl.pallas_call(
        functools.partial(_inproj_kernel, q_scale=Q_SCALE),
        grid=(b, nt),
        in_specs=in_specs,
        out_specs=out_specs,
        out_shape=out_shape,
        compiler_params=pltpu.CompilerParams(
            dimension_semantics=("parallel", "parallel"),
            vmem_limit_bytes=V7X_VMEM_LIMIT_BYTES),
        name="inproj",
    )(x, x, x, mod, norm_g.reshape(1, d), w_in, conv_w,
      jnp.tile(q_norm_g, 2).reshape(1, HEAD_W), jnp.tile(k_norm_g, 2).reshape(1, HEAD_W),
      *tables, w_conv_out)


def _attn_kernel(bound_ref, q_ref, k_ref, v_ref, sz_ref, lq1_ref, lk1_ref, lq2_ref, lk2_ref, sg_ref,
                 o_ref, acc_ref, m_ref, qz_ref, lam_ref, *, tk, tk_fast, lam_init):
    tq = q_ref.shape[0]
    seq = k_ref.shape[0]
    q = q_ref[...]
    first = lax.broadcasted_iota(jnp.int32, (tq, HEAD_W), 1) < HEAD_DIM
    zero = jnp.zeros_like(q)
    qz_ref[0:tq, :] = jnp.where(first, q, zero)
    qz_ref[tq:2 * tq, :] = jnp.where(first, zero, q)

    fast = bound_ref[0] <= MAX_UNSHIFTED_SCORE

    def store_lambda():
        lam = (jnp.exp(jnp.sum(lq1_ref[...] * lk1_ref[...], axis=1, keepdims=True))
               - jnp.exp(jnp.sum(lq2_ref[...] * lk2_ref[...], axis=1, keepdims=True)) + lam_init)
        lam_ref[...] = jnp.broadcast_to(lam, lam_ref.shape)

    def scores(j, width):
        return lax.dot_general(qz_ref[...], k_ref[j * width:(j + 1) * width, :], (((1,), (1,)), ((), ())),
                               preferred_element_type=F32)

    @pl.when(fast)
    def _():
        store_lambda()
        n = seq // tk_fast
        acc = None
        s_next = scores(0, tk_fast)
        for j in range(n):
            s_cur = s_next
            if j + 1 < n:
                s_next = scores(j + 1, tk_fast)
            p = jnp.exp2(s_cur).astype(BF16)
            d = jnp.dot(p, v_ref[j * tk_fast:(j + 1) * tk_fast, :], preferred_element_type=F32)
            acc = d if acc is None else acc + d
        acc_ref[...] = acc

    @pl.when(jnp.logical_not(fast))
    def _():
        store_lambda()
        n = seq // tk
        m_ref[...] = jnp.full(m_ref.shape, -jnp.inf, F32)
        acc_ref[...] = jnp.zeros(acc_ref.shape, F32)

        def accumulate(j, s):
            m_prev = m_ref[...]
            m_new = jnp.maximum(m_prev, jnp.max(s, axis=1, keepdims=True))
            alpha = jnp.exp2(m_prev - m_new)
            p = jnp.exp2(s - m_new).astype(BF16)
            acc_ref[...] = alpha * acc_ref[...] + jnp.dot(p, v_ref[j * tk:(j + 1) * tk, :],
                                                          preferred_element_type=F32)
            m_ref[...] = m_new

        s_next = scores(0, tk)
        for j in range(n):
            s_cur = s_next
            if j + 1 < n:
                s_next = scores(j + 1, tk)
            accumulate(j, s_cur)

    acc = acc_ref[...]
    o = acc[:, :HEAD_W] / acc[:, HEAD_W:]
    od = o[:tq] - lam_ref[...] * o[tq:]
    ms = jnp.mean(od * od, axis=-1, keepdims=True)
    on = od * lax.rsqrt(ms + EPS) * sg_ref[...] * (1.0 - lam_init)
    o_ref[...] = (on * sz_ref[...].astype(F32)).astype(BF16)


def _attn_call(score_bound, q, k, v_ext, sz, lam_q1, lam_k1, lam_q2, lam_k2, subln_g, *, tq, tk, tk_fast,
               lam_init):
    b, s, _ = q.shape
    assert s % tq == 0 and s % tk == 0 and s % tk_fast == 0, (s, tq, tk, tk_fast)
    qtile = pl.BlockSpec((None, tq, HEAD_W), lambda bi, h, i: (bi, i, h))
    vec = lambda w: pl.BlockSpec((1, w), lambda bi, h, i: (0, 0))
    return pl.pallas_call(
        functools.partial(_attn_kernel, tk=tk, tk_fast=tk_fast, lam_init=lam_init),
        grid=(b, N_HEADS, s // tq),
        in_specs=[pl.BlockSpec(memory_space=pltpu.SMEM),
                  qtile,
                  pl.BlockSpec((None, s, HEAD_W), lambda bi, h, i: (bi, 0, h)),
                  pl.BlockSpec((None, s, 2 * HEAD_W), lambda bi, h, i: (bi, 0, h)),
                  qtile,
                  vec(HEAD_DIM), vec(HEAD_DIM), vec(HEAD_DIM), vec(HEAD_DIM), vec(HEAD_W)],
        out_specs=qtile,
        out_shape=jax.ShapeDtypeStruct((b, s, D_ATTN), BF16),
        scratch_shapes=[pltpu.VMEM((2 * tq, 2 * HEAD_W), F32), pltpu.VMEM((2 * tq, 1), F32),
                        pltpu.VMEM((2 * tq, HEAD_W), BF16), pltpu.VMEM((1, HEAD_W), F32)],
        compiler_params=pltpu.CompilerParams(
            dimension_semantics=("parallel", "parallel", "parallel"),
            vmem_limit_bytes=V7X_VMEM_LIMIT_BYTES),
        name="diff_attn",
    )(score_bound, q, k, v_ext, sz, lam_q1.reshape(1, -1), lam_k1.reshape(1, -1), lam_q2.reshape(1, -1),
      lam_k2.reshape(1, -1), subln_g.reshape(1, -1))


def _out_kernel(x_ref, og_ref, a_ref, sgb_ref, mod_ref, wao_ref, wo_ref, y_ref):
    branch_b = jnp.dot(og_ref[...], wao_ref[...], preferred_element_type=F32)
    merged = a_ref[...].astype(F32) + sgb_ref[...].astype(F32) * branch_b
    z = jnp.dot(merged.astype(BF16), wo_ref[...], preferred_element_type=F32)
    y_ref[...] = x_ref[...] + mod_ref[2:3, :] * z


def _out_call(x, og, a, sgb, mod, w_attn_out, w_out, *, tm):
    b, s, d = x.shape
    tile = lambda w: pl.BlockSpec((None, tm, w), lambda bi, i: (bi, i, 0))
    const = lambda shape: pl.BlockSpec(shape, lambda bi, i: (0,) * len(shape))
    return pl.pallas_call(
        _out_kernel,
        grid=(b, s // tm),
        in_specs=[tile(d), tile(D_ATTN), tile(d), tile(d),
                  pl.BlockSpec((None, 3, d), lambda bi, i: (bi, 0, 0)),
                  const(w_attn_out.shape), const(w_out.shape)],
        out_specs=tile(d),
        out_shape=jax.ShapeDtypeStruct((b, s, d), F32),
        compiler_params=pltpu.CompilerParams(
            dimension_semantics=("parallel", "parallel"),
            vmem_limit_bytes=V7X_VMEM_LIMIT_BYTES),
        name="out_proj",
    )(x, og, a, sgb, mod, w_attn_out, w_out)


def _tiles(seq):
    if seq <= 2048:
        return dict(tm=min(seq, 512), to=min(seq, 1024), tq=min(seq, 1024), tk=min(seq, 1024), tk_fast=min(seq, 256))
    return dict(tm=512, to=1024, tq=256, tk=4096, tk_fast=256)


def _layer(x, mod, layer_idx, tables, norm_g, w_in, conv_w, q_norm_g, k_norm_g, lam_q1, lam_k1, lam_q2,
           lam_k2, subln_g, w_conv_out, w_attn_out, w_out):
    t = _tiles(x.shape[1])
    lam_init = 0.8 - 0.6 * math.exp(-0.3 * layer_idx)
    q, k, v_ext, sz, a, sgb = _inproj_call(x, mod, norm_g, w_in, conv_w, q_norm_g, k_norm_g, tables,
                                           w_conv_out, tm=t["tm"])
    score_bound = (HEAD_DIM * Q_SCALE * jnp.max(jnp.abs(q_norm_g)) * jnp.max(jnp.abs(k_norm_g))).reshape(1)
    og = _attn_call(score_bound.astype(F32), q, k, v_ext, sz, lam_q1, lam_k1, lam_q2, lam_k2, subln_g,
                    tq=t["tq"], tk=t["tk"], tk_fast=t["tk_fast"], lam_init=lam_init)
    return _out_call(x, og, a, sgb, mod, w_attn_out, w_out, tm=t["to"])


def kernel(x_prompt, x_sample, c_prompt, c_sample, norm_g, w_ada, b_ada, w_in, conv_w, q_norm_g, k_norm_g,
           lam_q1, lam_k1, lam_q2, lam_k2, subln_g, w_conv_out, w_attn_out, w_out):
    depth = norm_g.shape[0]
    groups = [(x_prompt, c_prompt), (x_sample, c_sample)]
    n_rows = sum(c.shape[0] for _, c in groups)
    pad = -n_rows % 8
    c_all = jnp.concatenate([c for _, c in groups] + [jnp.zeros((pad, D_MODEL), F32)], axis=0)
    tables = _rope_call(max(x.shape[1] for x, _ in groups))
    xs = [x for x, _ in groups]
    for l in range(depth):
        mod_all = _mod_call(c_all, w_ada[l], b_ada[l])
        w_in_b, wco_b, wao_b, wo_b = (w[l].astype(BF16) for w in (w_in, w_conv_out, w_attn_out, w_out))
        row = 0
        for gi, (_, c) in enumerate(groups):
            nb = c.shape[0]
            mod = mod_all[row:row + nb].reshape(nb, 3, D_MODEL)
            row += nb
            xs[gi] = _layer(xs[gi], mod, l, tables, norm_g[l], w_in_b, conv_w[l],
                            q_norm_g[l], k_norm_g[l], lam_q1[l], lam_k1[l], lam_q2[l], lam_k2[l],
                            subln_g[l], wco_b, wao_b, wo_b)
    return tuple(xs)
```

```python
import functools
import math

import jax
import jax.numpy as jnp
from jax import lax
from jax.experimental import pallas as pl
from jax.experimental.pallas import tpu as pltpu

F32 = jnp.float32
BF16 = jnp.bfloat16

D_MODEL = 1024
D_CONV = 512
N_HEADS = 4
HEAD_DIM = 64
HEAD_W = 2 * HEAD_DIM
D_ATTN = N_HEADS * HEAD_W
ROT_DIM = HEAD_DIM // 4
ROT_HALF = ROT_DIM // 2
ROPE_THETA = 500000.0
EPS = 1e-6
LOG2E = 1.4426950408889634
Q_SCALE = HEAD_DIM ** -0.5 * LOG2E

_C_CONV = 0
_C_Q = 4 * D_CONV
_C_K = _C_Q + D_ATTN
_C_V = _C_K + D_ATTN
_C_AZ = _C_V + D_ATTN
_C_GA = _C_AZ + D_ATTN
_C_GB = _C_GA + D_MODEL

V7X_VMEM_LIMIT_BYTES = 56 * 1024 * 1024
HALO_ROWS = 8
ROPE_BLOCK = 128
MAX_UNSHIFTED_SCORE = 48.0


def _silu(z):
    return z * jax.nn.sigmoid(z)


def _mod_kernel(c_ref, w_ref, b_ref, o_ref):
    s = _silu(c_ref[...])
    o_ref[...] = jnp.dot(s, w_ref[...], preferred_element_type=F32,
                         precision=lax.Precision.HIGHEST) + b_ref[...]


def _mod_call(c_all, w_ada, b_ada):
    rows, d = c_all.shape
    n = w_ada.shape[1]
    tn = 1024
    return pl.pallas_call(
        _mod_kernel,
        grid=(n // tn,),
        in_specs=[pl.BlockSpec((rows, d), lambda j: (0, 0)),
                  pl.BlockSpec((d, tn), lambda j: (0, j)),
                  pl.BlockSpec((1, tn), lambda j: (0, j))],
        out_specs=pl.BlockSpec((rows, tn), lambda j: (0, j)),
        out_shape=jax.ShapeDtypeStruct((rows, n), F32),
        name="adaln_mod",
    )(c_all, w_ada, b_ada.reshape(1, n))


def _rope_kernel(inv_ref, m1_ref, m2_ref, cos_ref, sa_ref, sb_ref, clo_ref, slo_ref, chi_ref, shi_ref):
    i = pl.program_id(0)
    blocks = cos_ref.shape[0] // ROPE_BLOCK

    @pl.when(i == 0)
    def _():
        inv = inv_ref[...]
        lo = lax.broadcasted_iota(jnp.int32, clo_ref.shape, 0).astype(F32) * inv
        clo_ref[...] = jnp.cos(lo)
        slo_ref[...] = jnp.sin(lo)
        hi = (lax.broadcasted_iota(jnp.int32, chi_ref.shape, 0) * ROPE_BLOCK).astype(F32) * inv
        chi_ref[...] = jnp.cos(hi)
        shi_ref[...] = jnp.sin(hi)

    clo, slo = clo_ref[...], slo_ref[...]
    m1, m2 = m1_ref[...], m2_ref[...]
    for a in range(blocks):
        blk = i * blocks + a
        chi = chi_ref[pl.ds(blk, 1), :]
        shi = shi_ref[pl.ds(blk, 1), :]
        sin = shi * clo + chi * slo
        rows = slice(a * ROPE_BLOCK, (a + 1) * ROPE_BLOCK)
        cos_ref[rows, :] = chi * clo - shi * slo
        sa_ref[rows, :] = -sin * m1
        sb_ref[rows, :] = sin * m2


def _rope_call(seq):
    lane = jnp.arange(HEAD_W)
    sub = lane % HEAD_DIM
    inv8 = ROPE_THETA ** (-jnp.arange(ROT_HALF, dtype=F32) / ROT_HALF)
    inv = jnp.where(sub < ROT_DIM, inv8[sub % ROT_HALF], 0.0).astype(F32).reshape(1, HEAD_W)
    m1 = (sub < ROT_HALF).astype(F32).reshape(1, HEAD_W)
    m2 = ((sub >= ROT_HALF) & (sub < ROT_DIM)).astype(F32).reshape(1, HEAD_W)
    ts = min(seq, 1024)
    assert seq % ts == 0 and ts % ROPE_BLOCK == 0, (seq, ts)
    vec = pl.BlockSpec((1, HEAD_W), lambda i: (0, 0))
    tab = pl.BlockSpec((ts, HEAD_W), lambda i: (i, 0))
    return pl.pallas_call(
        _rope_kernel,
        grid=(seq // ts,),
        in_specs=[vec, vec, vec],
        out_specs=[tab, tab, tab],
        out_shape=[jax.ShapeDtypeStruct((seq, HEAD_W), F32)] * 3,
        scratch_shapes=[pltpu.VMEM((ROPE_BLOCK, HEAD_W), F32), pltpu.VMEM((ROPE_BLOCK, HEAD_W), F32),
                        pltpu.VMEM((seq // ROPE_BLOCK, HEAD_W), F32), pltpu.VMEM((seq // ROPE_BLOCK, HEAD_W), F32)],
        compiler_params=pltpu.CompilerParams(dimension_semantics=("arbitrary",)),
        name="rope_tables",
    )(inv, m1, m2)


def _group_mean_sq(x):
    sq = x * x
    first = lax.broadcasted_iota(jnp.int32, x.shape, 1) < HEAD_DIM
    s_first = jnp.sum(jnp.where(first, sq, 0.0), axis=1, keepdims=True)
    s_second = jnp.sum(jnp.where(first, 0.0, sq), axis=1, keepdims=True)
    return jnp.where(first, s_first, s_second) * (1.0 / HEAD_DIM)


def _qk_norm_rope(x, gain, cos, sa, sb):
    xn = x * lax.rsqrt(_group_mean_sq(x) + EPS) * gain
    return xn * cos + pltpu.roll(xn, HEAD_W - ROT_HALF, 1) * sa + pltpu.roll(xn, ROT_HALF, 1) * sb


def _inproj_kernel(x_ref, xp_ref, xn_ref, mod_ref, ng_ref, win_ref, convw_ref, qg_ref, kg_ref,
                   cos_ref, sa_ref, sb_ref, wco_ref,
                   q_ref, k_ref, v_ref, sz_ref, a_ref, sgb_ref, *, q_scale):
    i = pl.program_id(1)
    last = pl.num_programs(1) - 1
    tm = x_ref.shape[0]
    shift = mod_ref[0:1, :]
    scale1 = 1.0 + mod_ref[1:2, :]
    ng = ng_ref[...]

    def norm_mod(x):
        ms = jnp.mean(x * x, axis=-1, keepdims=True)
        return (x * lax.rsqrt(ms + EPS) * ng) * scale1 + shift

    h = norm_mod(x_ref[...]).astype(BF16)

    def proj(lo, width):
        return jnp.dot(h, win_ref[:, lo:lo + width], preferred_element_type=F32)

    pqk = proj(_C_Q, 2 * D_ATTN)
    cos, sa, sb = cos_ref[...], sa_ref[...], sb_ref[...]
    qg, kg = qg_ref[...], kg_ref[...]
    for hd in range(N_HEADS):
        lo = hd * HEAD_W
        qh = _qk_norm_rope(pqk[:, lo:lo + HEAD_W], qg, cos, sa, sb)
        q_ref[:, lo:lo + HEAD_W] = (qh * q_scale).astype(BF16)
        kh = _qk_norm_rope(pqk[:, D_ATTN + lo:D_ATTN + lo + HEAD_W], kg, cos, sa, sb)
        k_ref[:, lo:lo + HEAD_W] = kh.astype(BF16)

    hh = norm_mod(jnp.concatenate([xp_ref[...], xn_ref[...]], axis=0)).astype(BF16)
    pc = jnp.dot(jnp.concatenate([h, hh], axis=0), win_ref[:, _C_CONV:_C_CONV + 4 * D_CONV],
                 preferred_element_type=F32)
    cb, cc, cx, cz = (pc[:tm, j * D_CONV:(j + 1) * D_CONV] for j in range(4))
    u = cc * cx
    uh = pc[tm:, D_CONV:2 * D_CONV] * pc[tm:, 2 * D_CONV:3 * D_CONV]
    u_prev = jnp.where(i > 0, uh[HALO_ROWS - 1:HALO_ROWS, :], 0.0)
    u_next = jnp.where(i < last, uh[HALO_ROWS:HALO_ROWS + 1, :], 0.0)
    row = lax.broadcasted_iota(jnp.int32, (tm, D_CONV), 0)
    u_dn = jnp.where(row == 0, u_prev, pltpu.roll(u, 1, 0))
    u_up = jnp.where(row == tm - 1, u_next, pltpu.roll(u, tm - 1, 0))
    cw = convw_ref[...]
    y = cb * (cw[0:1, :] * u_dn + cw[1:2, :] * u + cw[2:3, :] * u_up) * _silu(cz)
    y = y.astype(BF16)

    sz_ref[...] = _silu(proj(_C_AZ, D_ATTN)).astype(BF16)

    sga = jax.nn.sigmoid(proj(_C_GA, D_MODEL))
    branch_a = jnp.dot(y, wco_ref[...], preferred_element_type=F32)
    a_ref[...] = (sga * branch_a).astype(BF16)
    sgb_ref[...] = jax.nn.sigmoid(proj(_C_GB, D_MODEL)).astype(BF16)

    pv = proj(_C_V, D_ATTN)
    ones = jnp.ones((tm, HEAD_W), BF16)
    for hd in range(N_HEADS):
        v_ref[:, 2 * hd * HEAD_W:(2 * hd + 1) * HEAD_W] = pv[:, hd * HEAD_W:(hd + 1) * HEAD_W].astype(BF16)
        v_ref[:, (2 * hd + 1) * HEAD_W:(2 * hd + 2) * HEAD_W] = ones


def _inproj_call(x, mod, norm_g, w_in, conv_w, q_norm_g, k_norm_g, tables, w_conv_out, *, tm):
    b, s, d = x.shape
    nt = s // tm
    hb = tm // HALO_ROWS
    tile = lambda w: pl.BlockSpec((None, tm, w), lambda bi, i: (bi, i, 0))
    const = lambda shape: pl.BlockSpec(shape, lambda bi, i: (0,) * len(shape), pipeline_mode=pl.Buffered(1))
    tab = pl.BlockSpec((tm, HEAD_W), lambda bi, i: (i, 0))
    in_specs = [
        tile(d),
        pl.BlockSpec((None, HALO_ROWS, d), lambda bi, i: (bi, jnp.maximum(i * hb - 1, 0), 0)),
        pl.BlockSpec((None, HALO_ROWS, d), lambda bi, i: (bi, jnp.minimum((i + 1) * hb, nt * hb - 1), 0)),
        pl.BlockSpec((None, 3, d), lambda bi, i: (bi, 0, 0)),
        const((1, d)),
        const(w_in.shape),
        const(conv_w.shape),
        const((1, HEAD_W)),
        const((1, HEAD_W)),
        tab, tab, tab,
        const(w_conv_out.shape),
    ]
    out_specs = [tile(D_ATTN), tile(D_ATTN), tile(2 * D_ATTN), tile(D_ATTN), tile(d), tile(d)]
    out_shape = [jax.ShapeDtypeStruct((b, s, w), BF16)
                 for w in (D_ATTN, D_ATTN, 2 * D_ATTN, D_ATTN, d, d)]
    return pl.pallas_call(
        functools.partial(_inproj_kernel, q_scale=Q_SCALE),
        grid=(b, nt),
        in_specs=in_specs,
        out_specs=out_specs,
        out_shape=out_shape,
        compiler_params=pltpu.CompilerParams(
            dimension_semantics=("parallel", "parallel"),
            vmem_limit_bytes=V7X_VMEM_LIMIT_BYTES),
        name="inproj",
    )(x, x, x, mod, norm_g.reshape(1, d), w_in, conv_w,
      jnp.tile(q_norm_g, 2).reshape(1, HEAD_W), jnp.tile(k_norm_g, 2).reshape(1, HEAD_W),
      *tables, w_conv_out)


def _attn_kernel(bound_ref, q_ref, k_ref, v_ref, sz_ref, lq1_ref, lk1_ref, lq2_ref, lk2_ref, sg_ref,
                 o_ref, acc_ref, m_ref, qz_ref, lam_ref, s0_ref, s1_ref, *, tk, tk_fast, lam_init):
    tq = q_ref.shape[0]
    seq = k_ref.shape[0]
    q = q_ref[...]
    first = lax.broadcasted_iota(jnp.int32, (tq, HEAD_W), 1) < HEAD_DIM
    zero = jnp.zeros_like(q)
    qz_ref[0:tq, :] = jnp.where(first, q, zero)
    qz_ref[tq:2 * tq, :] = jnp.where(first, zero, q)

    fast = bound_ref[0] <= MAX_UNSHIFTED_SCORE

    def store_lambda():
        lam = (jnp.exp(jnp.sum(lq1_ref[...] * lk1_ref[...], axis=1, keepdims=True))
               - jnp.exp(jnp.sum(lq2_ref[...] * lk2_ref[...], axis=1, keepdims=True)) + lam_init)
        lam_ref[...] = jnp.broadcast_to(lam, lam_ref.shape)

    def scores(j, width):
        return lax.dot_general(qz_ref[...], k_ref[j * width:(j + 1) * width, :], (((1,), (1,)), ((), ())),
                               preferred_element_type=F32)

    @pl.when(fast)
    def _():
        store_lambda()
        n = seq // tk_fast
        acc = None
        s_next = scores(0, tk_fast)
        for j in range(n):
            s_cur = s_next
            if j + 1 < n:
                s_next = scores(j + 1, tk_fast)
            p = jnp.exp2(s_cur).astype(BF16)
            d = jnp.dot(p, v_ref[j * tk_fast:(j + 1) * tk_fast, :], preferred_element_type=F32)
            acc = d if acc is None else acc + d
        acc_ref[...] = acc

    @pl.when(jnp.logical_not(fast))
    def _():
        store_lambda()
        n = seq // tk
        m_ref[...] = jnp.full(m_ref.shape, -jnp.inf, F32)
        acc_ref[...] = jnp.zeros(acc_ref.shape, F32)

        def chunk(j):
            return pl.ds(pl.multiple_of(j * tk, tk), tk)

        def scores_into(j, s_ref):
            s_ref[...] = lax.dot_general(qz_ref[...], k_ref[chunk(j), :], (((1,), (1,)), ((), ())),
                                         preferred_element_type=F32)

        def accumulate(j, s_ref):
            s = s_ref[...]
            m_prev = m_ref[...]
            m_new = jnp.maximum(m_prev, jnp.max(s, axis=1, keepdims=True))
            alpha = jnp.exp2(m_prev - m_new)
            p = jnp.exp2(s - m_new).astype(BF16)
            acc_ref[...] = alpha * acc_ref[...] + jnp.dot(p, v_ref[chunk(j), :], preferred_element_type=F32)
            m_ref[...] = m_new

        scores_into(0, s0_ref)

        def pair(jj, carry):
            j = 2 * jj
            scores_into(j + 1, s1_ref)
            accumulate(j, s0_ref)
            scores_into(jnp.minimum(j + 2, n - 1), s0_ref)
            accumulate(j + 1, s1_ref)
            return carry

        lax.fori_loop(0, n // 2, pair, 0)

    acc = acc_ref[...]
    o = acc[:, :HEAD_W] / acc[:, HEAD_W:]
    od = o[:tq] - lam_ref[...] * o[tq:]
    ms = jnp.mean(od * od, axis=-1, keepdims=True)
    on = od * lax.rsqrt(ms + EPS) * sg_ref[...] * (1.0 - lam_init)
    o_ref[...] = (on * sz_ref[...].astype(F32)).astype(BF16)


def _attn_call(score_bound, q, k, v_ext, sz, lam_q1, lam_k1, lam_q2, lam_k2, subln_g, *, tq, tk, tk_fast,
               lam_init):
    b, s, _ = q.shape
    assert s % tq == 0 and s % (2 * tk) == 0 and s % tk_fast == 0, (s, tq, tk, tk_fast)
    qtile = pl.BlockSpec((None, tq, HEAD_W), lambda bi, h, i: (bi, i, h))
    vec = lambda w: pl.BlockSpec((1, w), lambda bi, h, i: (0, 0))
    return pl.pallas_call(
        functools.partial(_attn_kernel, tk=tk, tk_fast=tk_fast, lam_init=lam_init),
        grid=(b, N_HEADS, s // tq),
        in_specs=[pl.BlockSpec(memory_space=pltpu.SMEM),
                  qtile,
                  pl.BlockSpec((None, s, HEAD_W), lambda bi, h, i: (bi, 0, h)),
                  pl.BlockSpec((None, s, 2 * HEAD_W), lambda bi, h, i: (bi, 0, h)),
                  qtile,
                  vec(HEAD_DIM), vec(HEAD_DIM), vec(HEAD_DIM), vec(HEAD_DIM), vec(HEAD_W)],
        out_specs=qtile,
        out_shape=jax.ShapeDtypeStruct((b, s, D_ATTN), BF16),
        scratch_shapes=[pltpu.VMEM((2 * tq, 2 * HEAD_W), F32), pltpu.VMEM((2 * tq, 1), F32),
                        pltpu.VMEM((2 * tq, HEAD_W), BF16), pltpu.VMEM((1, HEAD_W), F32),
                        pltpu.VMEM((2 * tq, tk), F32), pltpu.VMEM((2 * tq, tk), F32)],
        compiler_params=pltpu.CompilerParams(
            dimension_semantics=("parallel", "parallel", "parallel"),
            vmem_limit_bytes=V7X_VMEM_LIMIT_BYTES),
        name="diff_attn",
    )(score_bound, q, k, v_ext, sz, lam_q1.reshape(1, -1), lam_k1.reshape(1, -1), lam_q2.reshape(1, -1),
      lam_k2.reshape(1, -1), subln_g.reshape(1, -1))


def _out_kernel(x_ref, og_ref, a_ref, sgb_ref, mod_ref, wao_ref, wo_ref, y_ref):
    branch_b = jnp.dot(og_ref[...], wao_ref[...], preferred_element_type=F32)
    merged = a_ref[...].astype(F32) + sgb_ref[...].astype(F32) * branch_b
    z = jnp.dot(merged.astype(BF16), wo_ref[...], preferred_element_type=F32)
    y_ref[...] = x_ref[...] + mod_ref[2:3, :] * z


def _out_call(x, og, a, sgb, mod, w_attn_out, w_out, *, tm):
    b, s, d = x.shape
    tile = lambda w: pl.BlockSpec((None, tm, w), lambda bi, i: (bi, i, 0))
    const = lambda shape: pl.BlockSpec(shape, lambda bi, i: (0,) * len(shape))
    return pl.pallas_call(
        _out_kernel,
        grid=(b, s // tm),
        in_specs=[tile(d), tile(D_ATTN), tile(d), tile(d),
                  pl.BlockSpec((None, 3, d), lambda bi, i: (bi, 0, 0)),
                  const(w_attn_out.shape), const(w_out.shape)],
        out_specs=tile(d),
        out_shape=jax.ShapeDtypeStruct((b, s, d), F32),
        compiler_params=pltpu.CompilerParams(
            dimension_semantics=("parallel", "parallel"),
            vmem_limit_bytes=V7X_VMEM_LIMIT_BYTES),
        name="out_proj",
    )(x, og, a, sgb, mod, w_attn_out, w_out)


def _tiles(seq):
    if seq <= 2048:
        return dict(tm=min(seq, 512), to=min(seq, 1024), tq=min(seq, 1024), tk=min(seq // 2, 1024), tk_fast=min(seq, 256))
    return dict(tm=512, to=1024, tq=512, tk=1024, tk_fast=256)


def _layer(x, mod, layer_idx, tables, norm_g, w_in, conv_w, q_norm_g, k_norm_g, lam_q1, lam_k1, lam_q2,
           lam_k2, subln_g, w_conv_out, w_attn_out, w_out):
    t = _tiles(x.shape[1])
    lam_init = 0.8 - 0.6 * math.exp(-0.3 * layer_idx)
    q, k, v_ext, sz, a, sgb = _inproj_call(x, mod, norm_g, w_in, conv_w, q_norm_g, k_norm_g, tables,
                                           w_conv_out, tm=t["tm"])
    score_bound = (HEAD_DIM * Q_SCALE * jnp.max(jnp.abs(q_norm_g)) * jnp.max(jnp.abs(k_norm_g))).reshape(1)
    og = _attn_call(score_bound.astype(F32), q, k, v_ext, sz, lam_q1, lam_k1, lam_q2, lam_k2, subln_g,
                    tq=t["tq"], tk=t["tk"], tk_fast=t["tk_fast"], lam_init=lam_init)
    return _out_call(x, og, a, sgb, mod, w_attn_out, w_out, tm=t["to"])


def kernel(x_prompt, x_sample, c_prompt, c_sample, norm_g, w_ada, b_ada, w_in, conv_w, q_norm_g, k_norm_g,
           lam_q1, lam_k1, lam_q2, lam_k2, subln_g, w_conv_out, w_attn_out, w_out):
    depth = norm_g.shape[0]
    groups = [(x_prompt, c_prompt), (x_sample, c_sample)]
    n_rows = sum(c.shape[0] for _, c in groups)
    pad = -n_rows % 8
    c_all = jnp.concatenate([c for _, c in groups] + [jnp.zeros((pad, D_MODEL), F32)], axis=0)
    tables = _rope_call(max(x.shape[1] for x, _ in groups))
    xs = [x for x, _ in groups]
    for l in range(depth):
        mod_all = _mod_call(c_all, w_ada[l], b_ada[l])
        w_in_b, wco_b, wao_b, wo_b = (w[l].astype(BF16) for w in (w_in, w_conv_out, w_attn_out, w_out))
        row = 0
        for gi, (_, c) in enumerate(groups):
            nb = c.shape[0]
            mod = mod_all[row:row + nb].reshape(nb, 3, D_MODEL)
            row += nb
            xs[gi] = _layer(xs[gi], mod, l, tables, norm_g[l], w_in_b, conv_w[l],
                            q_norm_g[l], k_norm_g[l], lam_q1[l], lam_k1[l], lam_q2[l], lam_k2[l],
                            subln_g[l], wco_b, wao_b, wo_b)
    return tuple(xs)
```

```python
import functools
import math

import jax
import jax.numpy as jnp
from jax import lax
from jax.experimental import pallas as pl
from jax.experimental.pallas import tpu as pltpu

F32 = jnp.float32
BF16 = jnp.bfloat16

D_MODEL = 1024
D_CONV = 512
N_HEADS = 4
HEAD_DIM = 64
HEAD_W = 2 * HEAD_DIM
D_ATTN = N_HEADS * HEAD_W
ROT_DIM = HEAD_DIM // 4
ROT_HALF = ROT_DIM // 2
ROPE_THETA = 500000.0
EPS = 1e-6
LOG2E = 1.4426950408889634
Q_SCALE = HEAD_DIM ** -0.5 * LOG2E

_C_CONV = 0
_C_Q = 4 * D_CONV
_C_K = _C_Q + D_ATTN
_C_V = _C_K + D_ATTN
_C_AZ = _C_V + D_ATTN
_C_GA = _C_AZ + D_ATTN
_C_GB = _C_GA + D_MODEL

V7X_VMEM_LIMIT_BYTES = 56 * 1024 * 1024
HALO_ROWS = 8
ROPE_BLOCK = 128
MAX_UNSHIFTED_SCORE = 48.0


def _silu(z):
    return z * jax.nn.sigmoid(z)


def _mod_kernel(c_ref, w_ref, b_ref, o_ref):
    s = _silu(c_ref[...])
    o_ref[...] = jnp.dot(s, w_ref[...], preferred_element_type=F32,
                         precision=lax.Precision.HIGHEST) + b_ref[...]


def _mod_call(c_all, w_ada, b_ada):
    rows, d = c_all.shape
    n = w_ada.shape[1]
    tn = 1024
    return pl.pallas_call(
        _mod_kernel,
        grid=(n // tn,),
        in_specs=[pl.BlockSpec((rows, d), lambda j: (0, 0)),
                  pl.BlockSpec((d, tn), lambda j: (0, j)),
                  pl.BlockSpec((1, tn), lambda j: (0, j))],
        out_specs=pl.BlockSpec((rows, tn), lambda j: (0, j)),
        out_shape=jax.ShapeDtypeStruct((rows, n), F32),
        name="adaln_mod",
    )(c_all, w_ada, b_ada.reshape(1, n))


def _rope_kernel(inv_ref, m1_ref, m2_ref, cos_ref, sa_ref, sb_ref, clo_ref, slo_ref, chi_ref, shi_ref):
    i = pl.program_id(0)
    blocks = cos_ref.shape[0] // ROPE_BLOCK

    @pl.when(i == 0)
    def _():
        inv = inv_ref[...]
        lo = lax.broadcasted_iota(jnp.int32, clo_ref.shape, 0).astype(F32) * inv
        clo_ref[...] = jnp.cos(lo)
        slo_ref[...] = jnp.sin(lo)
        hi = (lax.broadcasted_iota(jnp.int32, chi_ref.shape, 0) * ROPE_BLOCK).astype(F32) * inv
        chi_ref[...] = jnp.cos(hi)
        shi_ref[...] = jnp.sin(hi)

    clo, slo = clo_ref[...], slo_ref[...]
    m1, m2 = m1_ref[...], m2_ref[...]
    for a in range(blocks):
        blk = i * blocks + a
        chi = chi_ref[pl.ds(blk, 1), :]
        shi = shi_ref[pl.ds(blk, 1), :]
        sin = shi * clo + chi * slo
        rows = slice(a * ROPE_BLOCK, (a + 1) * ROPE_BLOCK)
        cos_ref[rows, :] = chi * clo - shi * slo
        sa_ref[rows, :] = -sin * m1
        sb_ref[rows, :] = sin * m2


def _rope_call(seq):
    lane = jnp.arange(HEAD_W)
    sub = lane % HEAD_DIM
    inv8 = ROPE_THETA ** (-jnp.arange(ROT_HALF, dtype=F32) / ROT_HALF)
    inv = jnp.where(sub < ROT_DIM, inv8[sub % ROT_HALF], 0.0).astype(F32).reshape(1, HEAD_W)
    m1 = (sub < ROT_HALF).astype(F32).reshape(1, HEAD_W)
    m2 = ((sub >= ROT_HALF) & (sub < ROT_DIM)).astype(F32).reshape(1, HEAD_W)
    ts = min(seq, 1024)
    assert seq % ts == 0 and ts % ROPE_BLOCK == 0, (seq, ts)
    vec = pl.BlockSpec((1, HEAD_W), lambda i: (0, 0))
    tab = pl.BlockSpec((ts, HEAD_W), lambda i: (i, 0))
    return pl.pallas_call(
        _rope_kernel,
        grid=(seq // ts,),
        in_specs=[vec, vec, vec],
        out_specs=[tab, tab, tab],
        out_shape=[jax.ShapeDtypeStruct((seq, HEAD_W), F32)] * 3,
        scratch_shapes=[pltpu.VMEM((ROPE_BLOCK, HEAD_W), F32), pltpu.VMEM((ROPE_BLOCK, HEAD_W), F32),
                        pltpu.VMEM((seq // ROPE_BLOCK, HEAD_W), F32), pltpu.VMEM((seq // ROPE_BLOCK, HEAD_W), F32)],
        compiler_params=pltpu.CompilerParams(dimension_semantics=("arbitrary",)),
        name="rope_tables",
    )(inv, m1, m2)


def _group_mean_sq(x):
    sq = x * x
    first = lax.broadcasted_iota(jnp.int32, x.shape, 1) < HEAD_DIM
    s_first = jnp.sum(jnp.where(first, sq, 0.0), axis=1, keepdims=True)
    s_second = jnp.sum(jnp.where(first, 0.0, sq), axis=1, keepdims=True)
    return jnp.where(first, s_first, s_second) * (1.0 / HEAD_DIM)


def _qk_norm_rope(x, gain, cos, sa, sb):
    xn = x * lax.rsqrt(_group_mean_sq(x) + EPS) * gain
    return xn * cos + pltpu.roll(xn, HEAD_W - ROT_HALF, 1) * sa + pltpu.roll(xn, ROT_HALF, 1) * sb


def _inproj_kernel(x_ref, xp_ref, xn_ref, mod_ref, ng_ref, win_ref, convw_ref, qg_ref, kg_ref,
                   cos_ref, sa_ref, sb_ref, wco_ref,
                   q_ref, k_ref, v_ref, sz_ref, a_ref, sgb_ref, *, q_scale):
    i = pl.program_id(1)
    last = pl.num_programs(1) - 1
    tm = x_ref.shape[0]
    shift = mod_ref[0:1, :]
    scale1 = 1.0 + mod_ref[1:2, :]
    ng = ng_ref[...]

    def norm_mod(x):
        ms = jnp.mean(x * x, axis=-1, keepdims=True)
        return (x * lax.rsqrt(ms + EPS) * ng) * scale1 + shift

    h = norm_mod(x_ref[...]).astype(BF16)

    def proj(lo, width):
        return jnp.dot(h, win_ref[:, lo:lo + width], preferred_element_type=F32)

    pqk = proj(_C_Q, 2 * D_ATTN)
    cos, sa, sb = cos_ref[...], sa_ref[...], sb_ref[...]
    qg, kg = qg_ref[...], kg_ref[...]
    for hd in range(N_HEADS):
        lo = hd * HEAD_W
        qh = _qk_norm_rope(pqk[:, lo:lo + HEAD_W], qg, cos, sa, sb)
        q_ref[:, lo:lo + HEAD_W] = (qh * q_scale).astype(BF16)
        kh = _qk_norm_rope(pqk[:, D_ATTN + lo:D_ATTN + lo + HEAD_W], kg, cos, sa, sb)
        k_ref[:, lo:lo + HEAD_W] = kh.astype(BF16)

    hh = norm_mod(jnp.concatenate([xp_ref[...], xn_ref[...]], axis=0)).astype(BF16)
    pc = jnp.dot(jnp.concatenate([h, hh], axis=0), win_ref[:, _C_CONV:_C_CONV + 4 * D_CONV],
                 preferred_element_type=F32)
    cb, cc, cx, cz = (pc[:tm, j * D_CONV:(j + 1) * D_CONV] for j in range(4))
    u = cc * cx
    uh = pc[tm:, D_CONV:2 * D_CONV] * pc[tm:, 2 * D_CONV:3 * D_CONV]
    u_prev = jnp.where(i > 0, uh[HALO_ROWS - 1:HALO_ROWS, :], 0.0)
    u_next = jnp.where(i < last, uh[HALO_ROWS:HALO_ROWS + 1, :], 0.0)
    row = lax.broadcasted_iota(jnp.int32, (tm, D_CONV), 0)
    u_dn = jnp.where(row == 0, u_prev, pltpu.roll(u, 1, 0))
    u_up = jnp.where(row == tm - 1, u_next, pltpu.roll(u, tm - 1, 0))
    cw = convw_ref[...]
    y = cb * (cw[0:1, :] * u_dn + cw[1:2, :] * u + cw[2:3, :] * u_up) * _silu(cz)
    y = y.astype(BF16)

    sz_ref[...] = _silu(proj(_C_AZ, D_ATTN)).astype(BF16)

    sga = jax.nn.sigmoid(proj(_C_GA, D_MODEL))
    branch_a = jnp.dot(y, wco_ref[...], preferred_element_type=F32)
    a_ref[...] = (sga * branch_a).astype(BF16)
    sgb_ref[...] = jax.nn.sigmoid(proj(_C_GB, D_MODEL)).astype(BF16)

    pv = proj(_C_V, D_ATTN)
    ones = jnp.ones((tm, HEAD_W), BF16)
    for hd in range(N_HEADS):
        v_ref[:, 2 * hd * HEAD_W:(2 * hd + 1) * HEAD_W] = pv[:, hd * HEAD_W:(hd + 1) * HEAD_W].astype(BF16)
        v_ref[:, (2 * hd + 1) * HEAD_W:(2 * hd + 2) * HEAD_W] = ones


def _inproj_call(x, mod, norm_g, w_in, conv_w, q_norm_g, k_norm_g, tables, w_conv_out, *, tm):
    b, s, d = x.shape
    nt = s // tm
    hb = tm // HALO_ROWS
    tile = lambda w: pl.BlockSpec((None, tm, w), lambda bi, i: (bi, i, 0))
    const = lambda shape: pl.BlockSpec(shape, lambda bi, i: (0,) * len(shape), pipeline_mode=pl.Buffered(1))
    tab = pl.BlockSpec((tm, HEAD_W), lambda bi, i: (i, 0))
    in_specs = [
        tile(d),
        pl.BlockSpec((None, HALO_ROWS, d), lambda bi, i: (bi, jnp.maximum(i * hb - 1, 0), 0)),
        pl.BlockSpec((None, HALO_ROWS, d), lambda bi, i: (bi, jnp.minimum((i + 1) * hb, nt * hb - 1), 0)),
        pl.BlockSpec((None, 3, d), lambda bi, i: (bi, 0, 0)),
        const((1, d)),
        const(w_in.shape),
        const(conv_w.shape),
        const((1, HEAD_W)),
        const((1, HEAD_W)),
        tab, tab, tab,
        const(w_conv_out.shape),
    ]
    out_specs = [tile(D_ATTN), tile(D_ATTN), tile(2 * D_ATTN), tile(D_ATTN), tile(d), tile(d)]
    out_shape = [jax.ShapeDtypeStruct((b, s, w), BF16)
                 for w in (D_ATTN, D_ATTN, 2 * D_ATTN, D_ATTN, d, d)]
    return pl.pallas_call(
        functools.partial(_inproj_kernel, q_scale=Q_SCALE),
        grid=(b, nt),
        in_specs=in_specs,
        out_specs=out_specs,
        out_shape=out_shape,
        compiler_params=pltpu.CompilerParams(
            dimension_semantics=("parallel", "parallel"),
            vmem_limit_bytes=V7X_VMEM_LIMIT_BYTES),
        name="inproj",
    )(x, x, x, mod, norm_g.reshape(1, d), w_in, conv_w,
      jnp.tile(q_norm_g, 2).reshape(1, HEAD_W), jnp.tile(k_norm_g, 2).reshape(1, HEAD_W),
      *tables, w_conv_out)


def _attn_kernel(bound_ref, q_ref, k_ref, v_ref, sz_ref, lq1_ref, lk1_ref, lq2_ref, lk2_ref, sg_ref,
                 o_ref, acc_ref, m_ref, qz_ref, s0_ref, s1_ref, *, tk, tk_fast, lam_init):
    tq = q_ref.shape[0]
    seq = k_ref.shape[0]
    q = q_ref[...]
    first = lax.broadcasted_iota(jnp.int32, (tq, HEAD_W), 1) < HEAD_DIM
    zero = jnp.zeros_like(q)
    qz_ref[0:tq, :] = jnp.where(first, q, zero)
    qz_ref[tq:2 * tq, :] = jnp.where(first, zero, q)

    fast = bound_ref[0] <= MAX_UNSHIFTED_SCORE

    def finish(acc):
        lam = (jnp.exp(jnp.sum(lq1_ref[...] * lk1_ref[...], axis=1, keepdims=True))
               - jnp.exp(jnp.sum(lq2_ref[...] * lk2_ref[...], axis=1, keepdims=True)) + lam_init)
        o = acc[:, :HEAD_W] / acc[:, HEAD_W:]
        od = o[:tq] - lam * o[tq:]
        ms = jnp.mean(od * od, axis=-1, keepdims=True)
        on = od * lax.rsqrt(ms + EPS) * sg_ref[...] * (1.0 - lam_init)
        o_ref[...] = (on * sz_ref[...].astype(F32)).astype(BF16)

    def scores(j, width):
        return lax.dot_general(qz_ref[...], k_ref[j * width:(j + 1) * width, :], (((1,), (1,)), ((), ())),
                               preferred_element_type=F32)

    @pl.when(fast)
    def _():
        n = seq // tk_fast
        acc = None
        s_next = scores(0, tk_fast)
        for j in range(n):
            s_cur = s_next
            if j + 1 < n:
                s_next = scores(j + 1, tk_fast)
            p = jnp.exp2(s_cur).astype(BF16)
            d = jnp.dot(p, v_ref[j * tk_fast:(j + 1) * tk_fast, :], preferred_element_type=F32)
            acc = d if acc is None else acc + d
        finish(acc)

    @pl.when(jnp.logical_not(fast))
    def _():
        n = seq // tk
        m_ref[...] = jnp.full(m_ref.shape, -jnp.inf, F32)
        acc_ref[...] = jnp.zeros(acc_ref.shape, F32)

        def chunk(j):
            return pl.ds(pl.multiple_of(j * tk, tk), tk)

        def scores_into(j, s_ref):
            s_ref[...] = lax.dot_general(qz_ref[...], k_ref[chunk(j), :], (((1,), (1,)), ((), ())),
                                         preferred_element_type=F32)

        def accumulate(j, s_ref):
            s = s_ref[...]
            m_prev = m_ref[...]
            m_new = jnp.maximum(m_prev, jnp.max(s, axis=1, keepdims=True))
            alpha = jnp.exp2(m_prev - m_new)
            p = jnp.exp2(s - m_new).astype(BF16)
            acc_ref[...] = alpha * acc_ref[...] + jnp.dot(p, v_ref[chunk(j), :], preferred_element_type=F32)
            m_ref[...] = m_new

        scores_into(0, s0_ref)

        def pair(jj, carry):
            j = 2 * jj
            scores_into(j + 1, s1_ref)
            accumulate(j, s0_ref)
            scores_into(jnp.minimum(j + 2, n - 1), s0_ref)
            accumulate(j + 1, s1_ref)
            return carry

        lax.fori_loop(0, n // 2, pair, 0)
        finish(acc_ref[...])


def _attn_call(score_bound, q, k, v_ext, sz, lam_q1, lam_k1, lam_q2, lam_k2, subln_g, *, tq, tk, tk_fast,
               lam_init):
    b, s, _ = q.shape
    assert s % tq == 0 and s % (2 * tk) == 0 and s % tk_fast == 0, (s, tq, tk, tk_fast)
    qtile = pl.BlockSpec((None, tq, HEAD_W), lambda bi, h, i: (bi, i, h))
    vec = lambda w: pl.BlockSpec((1, w), lambda bi, h, i: (0, 0))
    return pl.pallas_call(
        functools.partial(_attn_kernel, tk=tk, tk_fast=tk_fast, lam_init=lam_init),
        grid=(b, N_HEADS, s // tq),
        in_specs=[pl.BlockSpec(memory_space=pltpu.SMEM),
                  qtile,
                  pl.BlockSpec((None, s, HEAD_W), lambda bi, h, i: (bi, 0, h)),
                  pl.BlockSpec((None, s, 2 * HEAD_W), lambda bi, h, i: (bi, 0, h)),
                  qtile,
                  vec(HEAD_DIM), vec(HEAD_DIM), vec(HEAD_DIM), vec(HEAD_DIM), vec(HEAD_W)],
        out_specs=qtile,
        out_shape=jax.ShapeDtypeStruct((b, s, D_ATTN), BF16),
        scratch_shapes=[pltpu.VMEM((2 * tq, 2 * HEAD_W), F32), pltpu.VMEM((2 * tq, 1), F32),
                        pltpu.VMEM((2 * tq, HEAD_W), BF16),
                        pltpu.VMEM((2 * tq, tk), F32), pltpu.VMEM((2 * tq, tk), F32)],
        compiler_params=pltpu.CompilerParams(
            dimension_semantics=("parallel", "parallel", "parallel"),
            vmem_limit_bytes=V7X_VMEM_LIMIT_BYTES),
        name="diff_attn",
    )(score_bound, q, k, v_ext, sz, lam_q1.reshape(1, -1), lam_k1.reshape(1, -1), lam_q2.reshape(1, -1),
      lam_k2.reshape(1, -1), subln_g.reshape(1, -1))


def _out_kernel(x_ref, og_ref, a_ref, sgb_ref, mod_ref, wao_ref, wo_ref, y_ref):
    branch_b = jnp.dot(og_ref[...], wao_ref[...], preferred_element_type=F32)
    merged = a_ref[...].astype(F32) + sgb_ref[...].astype(F32) * branch_b
    z = jnp.dot(merged.astype(BF16), wo_ref[...], preferred_element_type=F32)
    y_ref[...] = x_ref[...] + mod_ref[2:3, :] * z


def _out_call(x, og, a, sgb, mod, w_attn_out, w_out, *, tm):
    b, s, d = x.shape
    tile = lambda w: pl.BlockSpec((None, tm, w), lambda bi, i: (bi, i, 0))
    const = lambda shape: pl.BlockSpec(shape, lambda bi, i: (0,) * len(shape))
    return pl.pallas_call(
        _out_kernel,
        grid=(b, s // tm),
        in_specs=[tile(d), tile(D_ATTN), tile(d), tile(d),
                  pl.BlockSpec((None, 3, d), lambda bi, i: (bi, 0, 0)),
                  const(w_attn_out.shape), const(w_out.shape)],
        out_specs=tile(d),
        out_shape=jax.ShapeDtypeStruct((b, s, d), F32),
        compiler_params=pltpu.CompilerParams(
            dimension_semantics=("parallel", "parallel"),
            vmem_limit_bytes=V7X_VMEM_LIMIT_BYTES),
        name="out_proj",
    )(x, og, a, sgb, mod, w_attn_out, w_out)


def _tiles(seq):
    if seq <= 2048:
        return dict(tm=min(seq, 512), to=min(seq, 1024), tq=min(seq, 1024), tk=min(seq // 2, 1024), tk_fast=min(seq, 256))
    return dict(tm=512, to=1024, tq=512, tk=1024, tk_fast=256)


def _layer(x, mod, layer_idx, tables, norm_g, w_in, conv_w, q_norm_g, k_norm_g, lam_q1, lam_k1, lam_q2,
           lam_k2, subln_g, w_conv_out, w_attn_out, w_out):
    t = _tiles(x.shape[1])
    lam_init = 0.8 - 0.6 * math.exp(-0.3 * layer_idx)
    q, k, v_ext, sz, a, sgb = _inproj_call(x, mod, norm_g, w_in, conv_w, q_norm_g, k_norm_g, tables,
                                           w_conv_out, tm=t["tm"])
    score_bound = (HEAD_DIM * Q_SCALE * jnp.max(jnp.abs(q_norm_g)) * jnp.max(jnp.abs(k_norm_g))).reshape(1)
    og = _attn_call(score_bound.astype(F32), q, k, v_ext, sz, lam_q1, lam_k1, lam_q2, lam_k2, subln_g,
                    tq=t["tq"], tk=t["tk"], tk_fast=t["tk_fast"], lam_init=lam_init)
    return _out_call(x, og, a, sgb, mod, w_attn_out, w_out, tm=t["to"])


def kernel(x_prompt, x_sample, c_prompt, c_sample, norm_g, w_ada, b_ada, w_in, conv_w, q_norm_g, k_norm_g,
           lam_q1, lam_k1, lam_q2, lam_k2, subln_g, w_conv_out, w_attn_out, w_out):
    depth = norm_g.shape[0]
    groups = [(x_prompt, c_prompt), (x_sample, c_sample)]
    n_rows = sum(c.shape[0] for _, c in groups)
    pad = -n_rows % 8
    c_all = jnp.concatenate([c for _, c in groups] + [jnp.zeros((pad, D_MODEL), F32)], axis=0)
    tables = _rope_call(max(x.shape[1] for x, _ in groups))
    xs = [x for x, _ in groups]
    for l in range(depth):
        mod_all = _mod_call(c_all, w_ada[l], b_ada[l])
        w_in_b, wco_b, wao_b, wo_b = (w[l].astype(BF16) for w in (w_in, w_conv_out, w_attn_out, w_out))
        row = 0
        for gi, (_, c) in enumerate(groups):
            nb = c.shape[0]
            mod = mod_all[row:row + nb].reshape(nb, 3, D_MODEL)
            row += nb
            xs[gi] = _layer(xs[gi], mod, l, tables, norm_g[l], w_in_b, conv_w[l],
                            q_norm_g[l], k_norm_g[l], lam_q1[l], lam_k1[l], lam_q2[l], lam_k2[l],
                            subln_g[l], wco_b, wao_b, wo_b)
    return tuple(xs)
```
